```python
import math
import jax, jax.numpy as jnp
from jax import lax
import numpy as np

D_MODEL = 1024
BATCH = 8
SEQ = 2048
DEPTH = 2
DEC_BATCH = 32
DEC_SEQ = 1
PAST_LEN = 8192
PAGE_SIZE = 128

N_META = 16
N_A_LAYERS = DEPTH // 2
N_B_LAYERS = DEPTH - N_A_LAYERS
SSM_EXPAND = 2
D_INNER = SSM_EXPAND * D_MODEL
SSM_HEAD_DIM = 64
SSM_HEADS = D_INNER // SSM_HEAD_DIM
SSM_GROUPS = 4
SSM_HEADS_PER_GROUP = SSM_HEADS // SSM_GROUPS
SSM_STATE = 128
D_CONV = 4
CONV_DIM = D_INNER + 2 * SSM_GROUPS * SSM_STATE
IN_PROJ_DIM = 2 * D_INNER + 2 * SSM_GROUPS * SSM_STATE + SSM_HEADS
SSD_CHUNK = 128
DIFF_HEADS = 8
DIFF_QK_DIM = 64
DIFF_V_DIM = 2 * DIFF_QK_DIM
ROT_DIM = DIFF_QK_DIM // 4
ROPE_THETA = 500000.0
Q_BLOCK = 128
D_FF = ((8 * D_MODEL + 3 * 256 - 1) // (3 * 256)) * 256
DEEPNORM_ALPHA = (2 * DEPTH) ** 0.25
DEEPNORM_BETA = (8 * DEPTH) ** -0.25
NORM_EPS = 1e-5

kernel_name = "yoco_mamba2_diffattn_decoder_step"


def layer_norm(x, w, b):
    xf = x.astype(jnp.float32)
    mu = jnp.mean(xf, -1, keepdims=True)
    var = jnp.mean(jnp.square(xf - mu), -1, keepdims=True)
    return ((xf - mu) * lax.rsqrt(var + NORM_EPS)).astype(x.dtype) * w + b


def rms_norm(x, w):
    xf = x.astype(jnp.float32)
    return (xf * lax.rsqrt(jnp.mean(jnp.square(xf), -1, keepdims=True) + NORM_EPS)).astype(x.dtype) * w


def post_norm(x, sub, w, b):
    return layer_norm(DEEPNORM_ALPHA * x + sub, w, b)


def swiglu(x, w_gate, w_up, w_down):
    return (jax.nn.silu(x @ w_gate) * (x @ w_up)) @ w_down


def rotary(x, pos):
    half = ROT_DIM // 2
    inv_freq = ROPE_THETA ** (-jnp.arange(half, dtype=jnp.float32) * 2.0 / ROT_DIM)
    ang = pos.astype(jnp.float32)[:, None] * inv_freq[None, :]
    cos = jnp.cos(ang).astype(x.dtype)[None, :, None, None, :]
    sin = jnp.sin(ang).astype(x.dtype)[None, :, None, None, :]
    x1 = x[..., :half]
    x2 = x[..., half:ROT_DIM]
    return jnp.concatenate([x1 * cos - x2 * sin, x2 * cos + x1 * sin, x[..., ROT_DIM:]], -1)


def segsum(x):
    T = x.shape[-1]
    xr = jnp.broadcast_to(x[..., :, None], x.shape + (T,))
    xr = jnp.where(jnp.tril(jnp.ones((T, T), bool), -1), xr, 0.0)
    s = jnp.cumsum(xr, axis=-2)
    return jnp.where(jnp.tril(jnp.ones((T, T), bool)), s, -jnp.inf)


def ssd_chunk_scan(X, dt, A, Bm, Cm, h0, chunk):
    b, L = X.shape[:2]
    c = L // chunk
    G, R, P, N = SSM_GROUPS, SSM_HEADS_PER_GROUP, SSM_HEAD_DIM, SSM_STATE
    dtype = X.dtype
    Xc = (X * dt.astype(dtype)[..., None]).reshape(b, c, chunk, G, R, P)
    ac = (dt * A).reshape(b, c, chunk, G, R).transpose(0, 3, 4, 1, 2)
    Bc = Bm.reshape(b, c, chunk, G, N)
    Cc = Cm.reshape(b, c, chunk, G, N)
    a_cs = jnp.cumsum(ac, axis=-1)
    Lmat = jnp.exp(segsum(ac)).astype(dtype)
    CB = jnp.einsum('bclgn,bcsgn->bgcls', Cc, Bc)
    y_diag = jnp.einsum('bgcls,bgrcls,bcsgrp->bclgrp', CB, Lmat, Xc)
    decay_states = jnp.exp(a_cs[..., -1:] - a_cs).astype(dtype)
    states = jnp.einsum('bcsgn,bgrcs,bcsgrp->bcgrpn', Bc, decay_states, Xc)
    states = jnp.concatenate([h0.astype(dtype).reshape(b, 1, G, R, P, N), states], axis=1)
    chunk_tot = jnp.pad(a_cs[..., -1], ((0, 0), (0, 0), (0, 0), (1, 0)))
    decay_chunk = jnp.exp(segsum(chunk_tot)).astype(dtype)
    new_states = jnp.einsum('bgrzc,bcgrpn->bzgrpn', decay_chunk, states)
    y_off = jnp.einsum('bclgn,bcgrpn,bgrcl->bclgrp', Cc, new_states[:, :-1], jnp.exp(a_cs).astype(dtype))
    y = (y_diag + y_off).reshape(b, L, G * R, P)
    return y, new_states[:, -1].reshape(b, G * R, P, N)


def causal_conv(xBC, prev, conv_w, conv_b):
    full = jnp.concatenate([prev.astype(xBC.dtype), xBC], axis=1)
    out = lax.conv_general_dilated(full, conv_w[:, None, :], window_strides=(1,), padding='VALID',
                                   dimension_numbers=('NWC', 'WIO', 'NWC'),
                                   feature_group_count=CONV_DIM)
    return jax.nn.silu(out + conv_b), full[:, -(D_CONV - 1):]


def mamba2_mixer(h, conv_prev, ssm_prev, segments, w_in, conv_w, conv_b, dt_bias, A_log, D_skip, norm_w, w_out):
    b, L, _ = h.shape
    zxbcdt = h @ w_in
    z = zxbcdt[..., :D_INNER]
    xBC = zxbcdt[..., D_INNER:D_INNER + CONV_DIM]
    dt_raw = zxbcdt[..., D_INNER + CONV_DIM:]
    xBC, new_conv = causal_conv(xBC, conv_prev, conv_w, conv_b)
    GN = SSM_GROUPS * SSM_STATE
    xs = xBC[..., :D_INNER].reshape(b, L, SSM_HEADS, SSM_HEAD_DIM)
    Bm = xBC[..., D_INNER:D_INNER + GN].reshape(b, L, SSM_GROUPS, SSM_STATE)
    Cm = xBC[..., D_INNER + GN:].reshape(b, L, SSM_GROUPS, SSM_STATE)
    dt = jax.nn.softplus((dt_raw + dt_bias).astype(jnp.float32))
    A = -jnp.exp(A_log.astype(jnp.float32))
    state = ssm_prev
    ys = []
    start = 0
    for length, chunk in segments:
        y_seg, state = ssd_chunk_scan(xs[:, start:start + length], dt[:, start:start + length], A,
                                      Bm[:, start:start + length], Cm[:, start:start + length], state, chunk)
        ys.append(y_seg)
        start += length
    y = jnp.concatenate(ys, axis=1) + xs * D_skip[:, None]
    y = rms_norm(y.reshape(b, L, D_INNER) * jax.nn.silu(z), norm_w)
    return y @ w_out, new_conv, state


def shared_kv(h, pos, w_k, w_v):
    b, L, _ = h.shape
    k = rotary((h @ w_k).reshape(b, L, DIFF_HEADS, 2, DIFF_QK_DIM), pos)
    v = (h @ w_v).reshape(b, L, DIFF_HEADS, DIFF_V_DIM)
    return k, v


def diff_lambda(lam_vecs, lambda_init):
    lv = lam_vecs.astype(jnp.float32)
    return jnp.exp(jnp.sum(lv[0] * lv[1])) - jnp.exp(jnp.sum(lv[2] * lv[3])) + lambda_init


def diff_weights(s, lam):
    p = jax.nn.softmax(s, axis=-1)
    return p[:, :, 0] - lam * p[:, :, 1]


def diff_output(o, subln_w, w_o, lambda_init):
    b, L = o.shape[:2]
    o = rms_norm(o, subln_w) * (1.0 - lambda_init)
    return o.reshape(b, L, DIFF_HEADS * DIFF_V_DIM) @ w_o


def diff_attn_prompt(h, k, v, pos, w_q, lam_vecs, subln_w, w_o, lambda_init):
    b, L, _ = h.shape
    q = rotary((h @ w_q).reshape(b, L, DIFF_HEADS, 2, DIFF_QK_DIM), pos)
    lam = diff_lambda(lam_vecs, lambda_init)
    nb = -(-L // Q_BLOCK)
    Lp = nb * Q_BLOCK
    pad = Lp - L
    qp = jnp.pad(q, ((0, 0), (0, pad), (0, 0), (0, 0), (0, 0)))
    kp = jnp.pad(k, ((0, 0), (0, pad), (0, 0), (0, 0), (0, 0)))
    vp = jnp.pad(v, ((0, 0), (0, pad), (0, 0), (0, 0)))
    q_blocks = qp.reshape(b, nb, Q_BLOCK, DIFF_HEADS, 2, DIFF_QK_DIM).swapaxes(0, 1)
    key_idx = jnp.arange(Lp)
    scale = DIFF_QK_DIM ** -0.5

    def attend_block(args):
        qb, blk = args
        s = jnp.einsum('bqhcd,bkhcd->bhcqk', qb, kp).astype(jnp.float32) * scale
        q_idx = blk * Q_BLOCK + jnp.arange(Q_BLOCK)
        s = jnp.where(key_idx[None, :] <= q_idx[:, None], s, -jnp.inf)
        w = diff_weights(s, lam).astype(vp.dtype)
        return jnp.einsum('bhqk,bkhe->bqhe', w, vp)

    o = lax.map(attend_block, (q_blocks, jnp.arange(nb)))
    o = o.swapaxes(0, 1).reshape(b, Lp, DIFF_HEADS, DIFF_V_DIM)[:, :L]
    return diff_output(o, subln_w, w_o, lambda_init)


def diff_attn_sample(h, k_past, v_past, k_new, v_new, pos, w_q, lam_vecs, subln_w, w_o, lambda_init):
    b, T, _ = h.shape
    q = rotary((h @ w_q).reshape(b, T, DIFF_HEADS, 2, DIFF_QK_DIM), pos)
    lam = diff_lambda(lam_vecs, lambda_init)
    scale = DIFF_QK_DIM ** -0.5
    s_past = jnp.einsum('bqhcd,bkhcd->bhcqk', q, k_past).astype(jnp.float32) * scale
    s_new = jnp.einsum('bqhcd,bkhcd->bhcqk', q, k_new).astype(jnp.float32) * scale
    s_new = jnp.where(jnp.tril(jnp.ones((T, T), bool)), s_new, -jnp.inf)
    w = diff_weights(jnp.concatenate([s_past, s_new], axis=-1), lam).astype(v_new.dtype)
    P = k_past.shape[1]
    o = (jnp.einsum('bhqk,bkhe->bqhe', w[..., :P], v_past)
         + jnp.einsum('bhqk,bkhe->bqhe', w[..., P:], v_new))
    return diff_output(o, subln_w, w_o, lambda_init)


def setup_inputs(seed: int = 0) -> dict:
    key = jax.random.key(seed)
    ks = jax.random.split(key, 32)
    n_pages = PAST_LEN // PAGE_SIZE
    n_pool = (5 * DEC_BATCH * n_pages) // 4
    HQK = DIFF_HEADS * 2 * DIFF_QK_DIM
    HV = DIFF_HEADS * DIFF_V_DIM

    def nrm(k, shape, scale):
        return jax.random.normal(k, shape, jnp.float32) * scale

    dt0 = jnp.exp(jax.random.uniform(ks[11], (N_A_LAYERS, SSM_HEADS), jnp.float32,
                                     minval=math.log(1e-3), maxval=math.log(1e-1)))
    return {
        "x_prompt": nrm(ks[0], (BATCH, SEQ, D_MODEL), 1.0),
        "x_sample": nrm(ks[1], (DEC_BATCH, DEC_SEQ, D_MODEL), 1.0),
        "cache_k": nrm(ks[2], (n_pool, PAGE_SIZE, DIFF_HEADS, 2, DIFF_QK_DIM), 1.0),
        "cache_v": nrm(ks[3], (n_pool, PAGE_SIZE, DIFF_HEADS, DIFF_V_DIM), 1.0),
        "state_ssm": nrm(ks[4], (N_A_LAYERS, DEC_BATCH, SSM_HEADS, SSM_HEAD_DIM, SSM_STATE), 0.1),
        "state_conv": nrm(ks[5], (N_A_LAYERS, DEC_BATCH, D_CONV - 1, CONV_DIM), 1.0),
        "page_table": jax.random.permutation(ks[6], n_pool)[:DEC_BATCH * n_pages]
                      .reshape(DEC_BATCH, n_pages).astype(jnp.int32),
        "meta_tokens": nrm(ks[7], (N_META, D_MODEL), 1.0),
        "a_w_in": nrm(ks[8], (N_A_LAYERS, D_MODEL, IN_PROJ_DIM), D_MODEL ** -0.5),
        "a_conv_w": nrm(ks[9], (N_A_LAYERS, D_CONV, CONV_DIM), D_CONV ** -0.5),
        "a_conv_b": nrm(ks[10], (N_A_LAYERS, CONV_DIM), 0.02),
        "a_dt_bias": dt0 + jnp.log(-jnp.expm1(-dt0)),
        "a_A_log": jnp.log(jax.random.uniform(ks[12], (N_A_LAYERS, SSM_HEADS), jnp.float32, 1.0, 16.0)),
        "a_D": 1.0 + nrm(ks[13], (N_A_LAYERS, SSM_HEADS), 0.1),
        "a_norm_w": 1.0 + nrm(ks[14], (N_A_LAYERS, D_INNER), 0.02),
        "a_w_out": nrm(ks[15], (N_A_LAYERS, D_INNER, D_MODEL), D_INNER ** -0.5 * DEEPNORM_BETA),
        "kv_w_k": nrm(ks[16], (D_MODEL, HQK), D_MODEL ** -0.5),
        "kv_w_v": nrm(ks[17], (D_MODEL, HV), D_MODEL ** -0.5),
        "b_w_q": nrm(ks[18], (N_B_LAYERS, D_MODEL, HQK), D_MODEL ** -0.5),
        "b_lambda": nrm(ks[19], (N_B_LAYERS, 4, DIFF_QK_DIM), 0.1),
        "b_subln_w": 1.0 + nrm(ks[20], (N_B_LAYERS, DIFF_V_DIM), 0.02),
        "b_w_o": nrm(ks[21], (N_B_LAYERS, HV, D_MODEL), HV ** -0.5 * DEEPNORM_BETA),
        "ffn_w_gate": nrm(ks[22], (DEPTH, D_MODEL, D_FF), D_MODEL ** -0.5),
        "ffn_w_up": nrm(ks[23], (DEPTH, D_MODEL, D_FF), D_MODEL ** -0.5),
        "ffn_w_down": nrm(ks[24], (DEPTH, D_FF, D_MODEL), D_FF ** -0.5 * DEEPNORM_BETA),
        "ln_mix_w": 1.0 + nrm(ks[25], (DEPTH, D_MODEL), 0.02),
        "ln_mix_b": nrm(ks[26], (DEPTH, D_MODEL), 0.01),
        "ln_ffn_w": 1.0 + nrm(ks[27], (DEPTH, D_MODEL), 0.02),
        "ln_ffn_b": nrm(ks[28], (DEPTH, D_MODEL), 0.01),
    }


def reference(x_prompt, x_sample, cache_k, cache_v, state_ssm, state_conv, page_table, meta_tokens,
              a_w_in, a_conv_w, a_conv_b, a_dt_bias, a_A_log, a_D, a_norm_w, a_w_out,
              kv_w_k, kv_w_v, b_w_q, b_lambda, b_subln_w, b_w_o,
              ffn_w_gate, ffn_w_up, ffn_w_down, ln_mix_w, ln_mix_b, ln_ffn_w, ln_ffn_b):
    bp, seq, _ = x_prompt.shape
    bs, T, _ = x_sample.shape
    n_pages = page_table.shape[1]
    past_len = n_pages * PAGE_SIZE

    hp = jnp.concatenate([jnp.broadcast_to(meta_tokens.astype(x_prompt.dtype)[None], (bp, N_META, D_MODEL)),
                          x_prompt], axis=1)
    hs = x_sample
    L_tot = hp.shape[1]
    pos_p = jnp.arange(L_tot)
    pos_s = past_len + jnp.arange(T)
    segs_p = ((N_META, N_META), (seq, min(SSD_CHUNK, seq)))
    segs_s = ((T, T),)

    k_past = cache_k[page_table].reshape(bs, past_len, DIFF_HEADS, 2, DIFF_QK_DIM)
    v_past = cache_v[page_table].reshape(bs, past_len, DIFF_HEADS, DIFF_V_DIM)

    ssm_p_list, conv_p_list, ssm_s_list, conv_s_list = [], [], [], []
    for l in range(DEPTH):
        if l < N_A_LAYERS:
            params = (a_w_in[l], a_conv_w[l], a_conv_b[l], a_dt_bias[l], a_A_log[l], a_D[l], a_norm_w[l], a_w_out[l])
            conv0 = jnp.zeros((bp, D_CONV - 1, CONV_DIM), hp.dtype)
            ssm0 = jnp.zeros((bp, SSM_HEADS, SSM_HEAD_DIM, SSM_STATE), hp.dtype)
            mix_p, conv_p, ssm_p = mamba2_mixer(hp, conv0, ssm0, segs_p, *params)
            mix_s, conv_s, ssm_s = mamba2_mixer(hs, state_conv[l], state_ssm[l], segs_s, *params)
            ssm_p_list.append(ssm_p)
            conv_p_list.append(conv_p)
            ssm_s_list.append(ssm_s)
            conv_s_list.append(conv_s)
        else:
            j = l - N_A_LAYERS
            lambda_init = 0.8 - 0.6 * math.exp(-0.3 * l)
            mix_p = diff_attn_prompt(hp, k_prompt, v_prompt, pos_p, b_w_q[j], b_lambda[j], b_subln_w[j],
                                     b_w_o[j], lambda_init)
            mix_s = diff_attn_sample(hs, k_past, v_past, k_sample, v_sample, pos_s, b_w_q[j], b_lambda[j],
                                     b_subln_w[j], b_w_o[j], lambda_init)
        hp = post_norm(hp, mix_p, ln_mix_w[l], ln_mix_b[l])
        hs = post_norm(hs, mix_s, ln_mix_w[l], ln_mix_b[l])
        hp = post_norm(hp, swiglu(hp, ffn_w_gate[l], ffn_w_up[l], ffn_w_down[l]), ln_ffn_w[l], ln_ffn_b[l])
        hs = post_norm(hs, swiglu(hs, ffn_w_gate[l], ffn_w_up[l], ffn_w_down[l]), ln_ffn_w[l], ln_ffn_b[l])
        if l == N_A_LAYERS - 1:
            k_prompt, v_prompt = shared_kv(hp, pos_p, kv_w_k, kv_w_v)
            k_sample, v_sample = shared_kv(hs, pos_s, kv_w_k, kv_w_v)

    y_prompt = hp[:, N_META:]
    ssm_prompt = jnp.stack(ssm_p_list, 0)
    conv_prompt = jnp.stack(conv_p_list, 0)
    ssm_sample = jnp.stack(ssm_s_list, 0)
    conv_sample = jnp.stack(conv_s_list, 0)
    return (y_prompt, hs, k_prompt, v_prompt, ssm_prompt, conv_prompt, k_sample, v_sample, ssm_sample, conv_sample)
```

```python
import functools
import math

import jax
import jax.numpy as jnp
from jax import lax
from jax.experimental import pallas as pl
from jax.experimental.pallas import tpu as pltpu

F32 = jnp.float32
BF16 = jnp.bfloat16

D_MODEL = 1024
N_META = 16
D_INNER = 2048
SSM_HEADS = 32
SSM_HEAD_DIM = 64
SSM_GROUPS = 4
SSM_STATE = 128
D_CONV = 4
GN = SSM_GROUPS * SSM_STATE
CONV_DIM = D_INNER + 2 * GN
DIFF_HEADS = 8
DIFF_QK_DIM = 64
DIFF_V_DIM = 128
HQK = DIFF_HEADS * 2 * DIFF_QK_DIM
ROT_DIM = DIFF_QK_DIM // 4
ROPE_THETA = 500000.0
D_FF = 2816
DEPTH = 2
DEEPNORM_ALPHA = (2 * DEPTH) ** 0.25
NORM_EPS = 1e-5
PAGE_SIZE = 128

LANES = 128
SUBLANES = 8
VMEM_LIMIT_BYTES = 56 * 1024 * 1024

CHUNK = 128
PAD_ROWS = CHUNK - N_META
ATT_TQ = 256
DEC_PAGES_PER_STEP = 8


def _cparams(sem):
    return pltpu.CompilerParams(dimension_semantics=sem, vmem_limit_bytes=VMEM_LIMIT_BYTES)


def _const_spec(shape):
    nd = len(shape)
    return pl.BlockSpec(shape, lambda *_: (0,) * nd)


def _layer_norm(x, w, b):
    mu = jnp.mean(x, axis=-1, keepdims=True)
    xc = x - mu
    var = jnp.mean(xc * xc, axis=-1, keepdims=True)
    return xc * lax.rsqrt(var + NORM_EPS) * w + b


def _silu(x):
    return x * (1.0 / (1.0 + jnp.exp(-x)))


def _softplus(x):
    return jnp.maximum(x, 0.0) + jnp.log1p(jnp.exp(-jnp.abs(x)))


def _dot(a, b):
    return jnp.dot(a, b, preferred_element_type=F32)


def _dot_nt(a, b):
    return lax.dot_general(a, b, (((1,), (1,)), ((), ())), preferred_element_type=F32)


NCHUNK = 512


def _in_proj_kernel(x_ref, wz_ref, wx_ref, wdt_ref, z_ref, xbc_ref, dt_ref):
    xb = x_ref[...].astype(BF16)
    for n0 in range(0, D_INNER, NCHUNK):
        z_ref[:, n0:n0 + NCHUNK] = _dot(xb, wz_ref[:, n0:n0 + NCHUNK])
    for n0 in range(0, CONV_DIM, NCHUNK):
        xbc_ref[:, n0:n0 + NCHUNK] = _dot(xb, wx_ref[:, n0:n0 + NCHUNK])
    dt_ref[...] = _dot(xb, wdt_ref[...])


def _in_proj(x, wz, wx, wdt, tm):
    m = x.shape[0]
    row = lambda i: (i, 0)
    return pl.pallas_call(
        _in_proj_kernel,
        grid=(m // tm,),
        in_specs=[pl.BlockSpec((tm, D_MODEL), row), _const_spec(wz.shape), _const_spec(wx.shape),
                  _const_spec(wdt.shape)],
        out_specs=[pl.BlockSpec((tm, D_INNER), row), pl.BlockSpec((tm, CONV_DIM), row),
                   pl.BlockSpec((tm, LANES), row)],
        out_shape=[jax.ShapeDtypeStruct((m, D_INNER), F32), jax.ShapeDtypeStruct((m, CONV_DIM), F32),
                   jax.ShapeDtypeStruct((m, LANES), F32)],
        compiler_params=_cparams(("parallel",)),
        name="in_proj",
    )(x, wz, wx, wdt)


def _mamba_prompt_kernel(xbc_ref, z_ref, dtr_ref, cw_ref, cb_ref, dtb_ref, alog_ref, dexp_ref, nw_ref,
                         y_ref, conv_out_ref, ssm_out_ref,
                         ext_ref, st_ref, xc_ref, ybuf_ref):
    c = pl.program_id(1)
    n_chunks = pl.num_programs(1)

    @pl.when(c == 0)
    def _():
        ext_ref[0:SUBLANES, :] = jnp.zeros((SUBLANES, CONV_DIM), F32)
        st_ref[...] = jnp.zeros(st_ref.shape, F32)

    x_raw = xbc_ref[...]
    ext_ref[SUBLANES:SUBLANES + CHUNK, :] = x_raw
    conv = cb_ref[...] + cw_ref[3:4, :] * x_raw
    for j in range(1, D_CONV):
        conv = conv + cw_ref[3 - j:4 - j, :] * ext_ref[pl.ds(SUBLANES - j, CHUNK), :]
    xc_ref[...] = _silu(conv)
    ext_ref[0:SUBLANES, :] = x_raw[CHUNK - SUBLANES:CHUNK, :]

    @pl.when(c == n_chunks - 1)
    def _():
        conv_out_ref[0] = x_raw[CHUNK - SUBLANES:CHUNK, :]

    row = lax.broadcasted_iota(jnp.int32, (CHUNK, LANES), 0)
    col = lax.broadcasted_iota(jnp.int32, (CHUNK, LANES), 1)
    valid = jnp.logical_or(c > 0, row >= PAD_ROWS)
    dt = jnp.where(valid, _softplus(dtr_ref[...] + dtb_ref[...]), 0.0)
    a = dt * (-jnp.exp(alog_ref[...]))
    tri = (row >= col).astype(F32)
    cs = jnp.dot(tri, a, preferred_element_type=F32, precision=lax.Precision.HIGHEST)
    cs_t = cs.T
    dt_t = dt.T
    e_cs = jnp.exp(cs)
    causal = row >= col
    lane_lo = col < SSM_HEAD_DIM

    for g in range(SSM_GROUPS):
        b_g = xc_ref[:, D_INNER + g * SSM_STATE:D_INNER + (g + 1) * SSM_STATE]
        c_g = xc_ref[:, D_INNER + GN + g * SSM_STATE:D_INNER + GN + (g + 1) * SSM_STATE]
        c_gb = c_g.astype(BF16)
        b_gb = b_g.astype(BF16)
        cb = _dot_nt(c_gb, b_gb)
        bt = b_g.T
        g0 = g * (D_INNER // SSM_GROUPS)
        y_off_g = _dot(c_gb, st_ref[:, g0:g0 + D_INNER // SSM_GROUPS].astype(BF16))
        for jp in range(D_INNER // SSM_GROUPS // LANES):
            lo = g0 + jp * LANES
            h0 = lo // SSM_HEAD_DIM
            m_parts, mp_parts, dec_parts, ecol_parts = [], [], [], []
            for h in (h0, h0 + 1):
                colv = cs[:, h:h + 1]
                rowv = cs_t[h:h + 1, :]
                dtrow = dt_t[h:h + 1, :]
                last = cs[CHUNK - 1:CHUNK, h:h + 1]
                lmat = jnp.exp(jnp.where(causal, colv - rowv, -jnp.inf))
                m_parts.append((cb * lmat * dtrow).astype(BF16))
                mp_parts.append((bt * (jnp.exp(last - rowv) * dtrow)).astype(BF16))
                dec_parts.append(jnp.exp(last))
                ecol_parts.append(e_cs[:, h:h + 1])
            lhs = jnp.concatenate([jnp.concatenate(m_parts, axis=1),
                                   jnp.concatenate(mp_parts, axis=1)], axis=0)
            x_pair = xc_ref[:, lo:lo + LANES]
            rhs = jnp.concatenate([jnp.where(lane_lo, x_pair, 0.0).astype(BF16),
                                   jnp.where(lane_lo, 0.0, x_pair).astype(BF16)], axis=0)
            res = _dot(lhs, rhs)
            y_diag = res[0:CHUNK]
            d_state = res[CHUNK:2 * CHUNK]
            decay = jnp.where(lane_lo, dec_parts[0], dec_parts[1])
            ecol = jnp.where(lane_lo, ecol_parts[0], ecol_parts[1])
            st_ref[:, lo:lo + LANES] = st_ref[:, lo:lo + LANES] * decay + d_state
            y_pair = y_diag + y_off_g[:, jp * LANES:(jp + 1) * LANES] * ecol + x_pair * dexp_ref[:, lo:lo + LANES]
            ybuf_ref[:, lo:lo + LANES] = y_pair

    gated = ybuf_ref[...] * _silu(z_ref[...])
    ms = jnp.mean(gated * gated, axis=-1, keepdims=True)
    y_ref[...] = (gated * lax.rsqrt(ms + NORM_EPS) * nw_ref[...]).astype(y_ref.dtype)

    @pl.when(c == n_chunks - 1)
    def _():
        for jp in range(D_INNER // LANES):
            t = st_ref[:, jp * LANES:(jp + 1) * LANES].T
            ssm_out_ref[0, 2 * jp] = t[0:SSM_HEAD_DIM]
            ssm_out_ref[0, 2 * jp + 1] = t[SSM_HEAD_DIM:2 * SSM_HEAD_DIM]


def _mamba_prompt(xbc, z, dt_raw, conv_w, conv_b, dt_bias, a_log, d_exp, norm_w, nb, n_chunks):
    m = xbc.shape[0]
    blk = lambda b, c: (b * n_chunks + c, 0)
    return pl.pallas_call(
        _mamba_prompt_kernel,
        grid=(nb, n_chunks),
        in_specs=[pl.BlockSpec((CHUNK, CONV_DIM), blk), pl.BlockSpec((CHUNK, D_INNER), blk),
                  pl.BlockSpec((CHUNK, LANES), blk),
                  _const_spec(conv_w.shape), _const_spec(conv_b.shape), _const_spec(dt_bias.shape),
                  _const_spec(a_log.shape), _const_spec(d_exp.shape), _const_spec(norm_w.shape)],
        out_specs=[pl.BlockSpec((CHUNK, D_INNER), blk),
                   pl.BlockSpec((1, SUBLANES, CONV_DIM), lambda b, c: (b, 0, 0)),
                   pl.BlockSpec((1, SSM_HEADS, SSM_HEAD_DIM, SSM_STATE), lambda b, c: (b, 0, 0, 0))],
        out_shape=[jax.ShapeDtypeStruct((m, D_INNER), BF16),
                   jax.ShapeDtypeStruct((nb, SUBLANES, CONV_DIM), F32),
                   jax.ShapeDtypeStruct((nb, SSM_HEADS, SSM_HEAD_DIM, SSM_STATE), F32)],
        scratch_shapes=[pltpu.VMEM((SUBLANES + CHUNK, CONV_DIM), F32),
                        pltpu.VMEM((SSM_STATE, D_INNER), F32),
                        pltpu.VMEM((CHUNK, CONV_DIM), F32),
                        pltpu.VMEM((CHUNK, D_INNER), F32)],
        compiler_params=_cparams(("parallel", "arbitrary")),
        name="mamba_prompt",
    )(xbc, z, dt_raw, conv_w, conv_b, dt_bias, a_log, d_exp, norm_w)


def _conv_step_kernel(xbc_ref, prev_ref, cw_ref, cb_ref, dtr_ref, dtb_ref, alog_ref,
                      xc_ref, newconv_ref, dt_ref, decay_ref):
    x_raw = xbc_ref[...]
    conv = cb_ref[...] + cw_ref[3:4, :] * x_raw
    for k in range(D_CONV - 1):
        conv = conv + cw_ref[k:k + 1, :] * prev_ref[k]
    xc_ref[...] = _silu(conv)
    newconv_ref[0] = prev_ref[1]
    newconv_ref[1] = prev_ref[2]
    newconv_ref[2] = x_raw
    dt = _softplus(dtr_ref[...] + dtb_ref[...])
    dt_ref[...] = dt
    decay_ref[...] = jnp.exp(dt * (-jnp.exp(alog_ref[...])))


def _conv_step(xbc, prev, conv_w, conv_b, dt_raw, dt_bias, a_log):
    nb = xbc.shape[0]
    return pl.pallas_call(
        _conv_step_kernel,
        out_shape=[jax.ShapeDtypeStruct((nb, CONV_DIM), F32), jax.ShapeDtypeStruct((D_CONV - 1, nb, CONV_DIM), F32),
                   jax.ShapeDtypeStruct((nb, LANES), F32), jax.ShapeDtypeStruct((nb, LANES), F32)],
        compiler_params=pltpu.CompilerParams(vmem_limit_bytes=VMEM_LIMIT_BYTES),
        name="conv_step",
    )(xbc, prev, conv_w, conv_b, dt_raw, dt_bias, a_log)


def _ssd_step_kernel(dt_sm, decay_sm, x_ref, z_ref, b_ref, c_ref, h_ref, dsk_ref, nw_ref, y_ref, hout_ref):
    bi = pl.program_id(0)
    x_t = x_ref[0]
    lane = lax.broadcasted_iota(jnp.int32, x_t.shape, 1)
    y_t = jnp.zeros(x_t.shape, F32)
    for h in range(SSM_HEADS):
        g = h // (SSM_HEADS // SSM_GROUPS)
        dt = dt_sm[bi, h]
        decay = decay_sm[bi, h]
        b_row = b_ref[0, g:g + 1, :]
        c_row = c_ref[0, g:g + 1, :]
        x_col = x_t[:, h:h + 1]
        h0 = h_ref[0, h]
        cb = jnp.sum(c_row * b_row, axis=-1, keepdims=True)
        y_col = decay * jnp.sum(h0 * c_row, axis=-1, keepdims=True) + (cb * dt) * x_col
        hout_ref[0, h] = decay * h0 + (dt * x_col) * b_row
        y_t = jnp.where(lane == h, y_col, y_t)
    y = y_t + x_t * dsk_ref[...]
    gated = y * _silu(z_ref[0])
    ms = jnp.sum(jnp.sum(gated * gated, axis=-1, keepdims=True), axis=0, keepdims=True) / D_INNER
    y_ref[0] = gated * lax.rsqrt(ms + NORM_EPS) * nw_ref[...]


def _ssd_step(dt, decay, x3t, z3t, b3, c3, h0, d_skip_t, norm_w_t):
    nb = x3t.shape[0]
    per_b = lambda b: (b, 0, 0)
    ph = (1, SSM_HEAD_DIM, SSM_HEADS)
    smem = pl.BlockSpec(memory_space=pltpu.SMEM)
    return pl.pallas_call(
        _ssd_step_kernel,
        grid=(nb,),
        in_specs=[smem, smem, pl.BlockSpec(ph, per_b), pl.BlockSpec(ph, per_b),
                  pl.BlockSpec((1, SSM_GROUPS, SSM_STATE), per_b), pl.BlockSpec((1, SSM_GROUPS, SSM_STATE), per_b),
                  pl.BlockSpec((1, SSM_HEADS, SSM_HEAD_DIM, SSM_STATE), lambda b: (b, 0, 0, 0)),
                  _const_spec(d_skip_t.shape), _const_spec(norm_w_t.shape)],
        out_specs=[pl.BlockSpec(ph, per_b),
                   pl.BlockSpec((1, SSM_HEADS, SSM_HEAD_DIM, SSM_STATE), lambda b: (b, 0, 0, 0))],
        out_shape=[jax.ShapeDtypeStruct((nb, SSM_HEAD_DIM, SSM_HEADS), F32),
                   jax.ShapeDtypeStruct((nb, SSM_HEADS, SSM_HEAD_DIM, SSM_STATE), F32)],
        compiler_params=_cparams(("parallel",)),
        name="ssd_step",
    )(dt, decay, x3t, z3t, b3, c3, h0, d_skip_t, norm_w_t)


def _proj_ln_kernel(a_ref, r_ref, w_ref, lw_ref, lb_ref, o_ref):
    mix = _dot(a_ref[...], w_ref[...])
    o_ref[...] = _layer_norm(DEEPNORM_ALPHA * r_ref[...] + mix, lw_ref[...], lb_ref[...])


def _proj_ln(a, resid, w, ln_w, ln_b, tm):
    m, k = a.shape
    row = lambda i: (i, 0)
    return pl.pallas_call(
        _proj_ln_kernel,
        grid=(m // tm,),
        in_specs=[pl.BlockSpec((tm, k), row), pl.BlockSpec((tm, D_MODEL), row), _const_spec(w.shape),
                  _const_spec(ln_w.shape), _const_spec(ln_b.shape)],
        out_specs=pl.BlockSpec((tm, D_MODEL), row),
        out_shape=jax.ShapeDtypeStruct((m, D_MODEL), F32),
        compiler_params=_cparams(("parallel",)),
        name="proj_ln",
    )(a, resid, w, ln_w, ln_b)


FF_CHUNK = 256


def _ffn_ln_kernel(x_ref, wg_ref, wu_ref, wd_ref, lw_ref, lb_ref, o_ref):
    x = x_ref[...]
    xb = x.astype(BF16)
    acc = jnp.zeros(x.shape, F32)
    for f0 in range(0, D_FF, FF_CHUNK):
        gate = _dot(xb, wg_ref[:, f0:f0 + FF_CHUNK])
        up = _dot(xb, wu_ref[:, f0:f0 + FF_CHUNK])
        act = (_silu(gate) * up).astype(BF16)
        acc = acc + _dot(act, wd_ref[f0:f0 + FF_CHUNK, :])
    o_ref[...] = _layer_norm(DEEPNORM_ALPHA * x + acc, lw_ref[...], lb_ref[...])


def _ffn_ln(x, wg, wu, wd, ln_w, ln_b, tm):
    m = x.shape[0]
    row = lambda i: (i, 0)
    return pl.pallas_call(
        _ffn_ln_kernel,
        grid=(m // tm,),
        in_specs=[pl.BlockSpec((tm, D_MODEL), row), _const_spec(wg.shape), _const_spec(wu.shape),
                  _const_spec(wd.shape), _const_spec(ln_w.shape), _const_spec(ln_b.shape)],
        out_specs=pl.BlockSpec((tm, D_MODEL), row),
        out_shape=jax.ShapeDtypeStruct((m, D_MODEL), F32),
        compiler_params=_cparams(("parallel",)),
        name="ffn_ln",
    )(x, wg, wu, wd, ln_w, ln_b)


def _rotary(x, cos_t, sin_a, sin_b):
    half = ROT_DIM // 2
    parts = []
    for j in range(x.shape[1] // LANES):
        xb = x[:, j * LANES:(j + 1) * LANES]
        fwd = pltpu.roll(xb, LANES - half, 1)
        bwd = pltpu.roll(xb, half, 1)
        parts.append(xb * cos_t + fwd * sin_a + bwd * sin_b)
    return jnp.concatenate(parts, axis=1)


def _kvq_kernel(h_ref, wk_ref, wv_ref, wq_ref, cos_ref, sa_ref, sb_ref, k_ref, v_ref, kb_ref, vb_ref, qb_ref):
    hb = h_ref[...].astype(BF16)
    cos_t, sin_a, sin_b = cos_ref[...], sa_ref[...], sb_ref[...]
    k = _rotary(_dot(hb, wk_ref[...]), cos_t, sin_a, sin_b)
    k_ref[...] = k
    kb_ref[...] = k.astype(BF16)
    v = _dot(hb, wv_ref[...])
    v_ref[...] = v
    vb_ref[...] = v.astype(BF16)
    q = _rotary(_dot(hb, wq_ref[...]), cos_t, sin_a, sin_b)
    qb_ref[...] = q.astype(qb_ref.dtype)


def _kvq(h, wk, wv, wq, cos_t, sin_a, sin_b, tm, q_dtype):
    m = h.shape[0]
    row = lambda i: (i, 0)
    full = pl.BlockSpec((tm, D_MODEL), row)
    tab = pl.BlockSpec((tm, LANES), row)
    return pl.pallas_call(
        _kvq_kernel,
        grid=(m // tm,),
        in_specs=[full, _const_spec(wk.shape), _const_spec(wv.shape), _const_spec(wq.shape), tab, tab, tab],
        out_specs=[full, full, full, full, full],
        out_shape=[jax.ShapeDtypeStruct((m, D_MODEL), F32), jax.ShapeDtypeStruct((m, D_MODEL), F32),
                   jax.ShapeDtypeStruct((m, D_MODEL), BF16), jax.ShapeDtypeStruct((m, D_MODEL), BF16),
                   jax.ShapeDtypeStruct((m, D_MODEL), q_dtype)],
        compiler_params=_cparams(("parallel",)),
        name="kvq_proj",
    )(h, wk, wv, wq, cos_t, sin_a, sin_b)


def _rope_tables(pos):
    half = ROT_DIM // 2
    inv_freq = ROPE_THETA ** (-jnp.arange(half, dtype=F32) * 2.0 / ROT_DIM)
    ang = pos.astype(F32)[:, None] * inv_freq[None, :]
    cos, sin = jnp.cos(ang), jnp.sin(ang)
    n = pos.shape[0]
    rest = DIFF_QK_DIM - ROT_DIM
    cos_sub = jnp.concatenate([cos, cos, jnp.ones((n, rest), F32)], axis=1)
    sa_sub = jnp.concatenate([-sin, jnp.zeros((n, half + rest), F32)], axis=1)
    sb_sub = jnp.concatenate([jnp.zeros((n, half), F32), sin, jnp.zeros((n, rest), F32)], axis=1)
    rep = lambda t: jnp.concatenate([t, t], axis=1)
    return rep(cos_sub), rep(sa_sub), rep(sb_sub)


def _diff_lambda(lam_ref, lambda_init):
    lv = lam_ref[...]
    s1 = jnp.sum(lv[0:1] * lv[1:2], axis=-1, keepdims=True)
    s2 = jnp.sum(lv[2:3] * lv[3:4], axis=-1, keepdims=True)
    return jnp.exp(s1) - jnp.exp(s2) + lambda_init


def _attn_prompt_kernel(q_ref, k_ref, v_ref, lam_ref, sw_ref, o_ref, m_ref, l_ref, acc_ref,
                        *, lambda_init, seq_len):
    i = pl.program_id(2)
    n_q = pl.num_programs(2)
    tq = ATT_TQ
    q = q_ref[0].astype(F32)
    lane = lax.broadcasted_iota(jnp.int32, (tq, LANES), 1)
    q_sub = (jnp.where(lane < DIFF_QK_DIM, q, 0.0).astype(BF16),
             jnp.where(lane < DIFF_QK_DIM, 0.0, q).astype(BF16))

    m_ref[...] = jnp.full(m_ref.shape, -jnp.inf, F32)
    l_ref[...] = jnp.zeros(l_ref.shape, F32)
    acc_ref[...] = jnp.zeros(acc_ref.shape, F32)

    def block(k0, tk, masked):
        kb = k_ref[0, pl.ds(k0, tk), :]
        vb = v_ref[0, pl.ds(k0, tk), :]
        if masked:
            q_idx = i * tq + lax.broadcasted_iota(jnp.int32, (tq, tk), 0)
            k_idx = k0 + lax.broadcasted_iota(jnp.int32, (tq, tk), 1)
            keep = jnp.logical_and(k_idx <= q_idx, jnp.logical_or(k_idx >= PAD_ROWS, q_idx < PAD_ROWS))
        for c in range(2):
            s = _dot_nt(q_sub[c], kb)
            if masked:
                s = jnp.where(keep, s, -jnp.inf)
            m_old = m_ref[c]
            m_new = jnp.maximum(m_old, jnp.max(s, axis=-1, keepdims=True))
            alpha = jnp.exp(m_old - m_new)
            p = jnp.exp(s - m_new)
            l_ref[c] = alpha * l_ref[c] + jnp.sum(p, axis=-1, keepdims=True)
            acc_ref[c] = alpha * acc_ref[c] + _dot(p.astype(BF16), vb)
            m_ref[c] = m_new

    block(0, tq, True)

    def body(j, carry):
        block(pl.multiple_of(j * tq, tq), tq, False)
        return carry

    lax.fori_loop(1, i, body, 0)

    last_tk = seq_len - (seq_len // tq) * tq
    if last_tk == 0:
        @pl.when(i > 0)
        def _():
            block(pl.multiple_of(i * tq, tq), tq, True)
    else:
        @pl.when(jnp.logical_and(i > 0, i < n_q - 1))
        def _():
            block(pl.multiple_of(i * tq, tq), tq, True)

        @pl.when(jnp.logical_and(i > 0, i == n_q - 1))
        def _():
            block((seq_len // tq) * tq, last_tk, True)

    lam = _diff_lambda(lam_ref, lambda_init)
    o = acc_ref[0] / l_ref[0] - lam * (acc_ref[1] / l_ref[1])
    ms = jnp.mean(o * o, axis=-1, keepdims=True)
    o = o * lax.rsqrt(ms + NORM_EPS) * sw_ref[...] * (1.0 - lambda_init)
    o_ref[0] = o.astype(o_ref.dtype)


def _attn_prompt(q3, k3, v3, lam_vecs, subln_w, lambda_init):
    nb, seq_len, _ = q3.shape
    n_q = pl.cdiv(seq_len, ATT_TQ)
    kern = functools.partial(_attn_prompt_kernel, lambda_init=lambda_init, seq_len=seq_len)
    return pl.pallas_call(
        kern,
        grid=(nb, DIFF_HEADS, n_q),
        in_specs=[pl.BlockSpec((1, ATT_TQ, LANES), lambda b, h, i: (b, i, h)),
                  pl.BlockSpec((1, seq_len, LANES), lambda b, h, i: (b, 0, h)),
                  pl.BlockSpec((1, seq_len, LANES), lambda b, h, i: (b, 0, h)),
                  _const_spec(lam_vecs.shape), _const_spec(subln_w.shape)],
        out_specs=pl.BlockSpec((1, ATT_TQ, LANES), lambda b, h, i: (b, i, h)),
        out_shape=jax.ShapeDtypeStruct((nb, seq_len, DIFF_HEADS * DIFF_V_DIM), BF16),
        scratch_shapes=[pltpu.VMEM((2, ATT_TQ, 1), F32), pltpu.VMEM((2, ATT_TQ, 1), F32),
                        pltpu.VMEM((2, ATT_TQ, LANES), F32)],
        compiler_params=_cparams(("parallel", "parallel", "arbitrary")),
        name="attn_prompt",
    )(q3, k3, v3, lam_vecs, subln_w)


N_SUB = 2 * DIFF_HEADS


def _attn_decode_kernel(pt_ref, *refs, lambda_init):
    npg = DEC_PAGES_PER_STEP
    k_refs = refs[0:npg]
    v_refs = refs[npg:2 * npg]
    q_ref, kn_ref, vn_ref, rep_ref, lam_ref, sw_ref = refs[2 * npg:2 * npg + 6]
    o_ref = refs[2 * npg + 6]
    m_ref, l_ref, acc_ref = refs[2 * npg + 7:]
    j = pl.program_id(1)

    @pl.when(j == 0)
    def _():
        m_ref[...] = jnp.full(m_ref.shape, -jnp.inf, F32)
        l_ref[...] = jnp.zeros(l_ref.shape, F32)
        acc_ref[...] = jnp.zeros(acc_ref.shape, F32)

    row = lax.broadcasted_iota(jnp.int32, (N_SUB, HQK), 0)
    col = lax.broadcasted_iota(jnp.int32, (N_SUB, HQK), 1)
    row_head = row % DIFF_HEADS
    q_mask = (col // DIFF_QK_DIM) == (2 * row_head + row // DIFF_HEADS)
    p_mask = (col % DIFF_HEADS) == row_head
    q_bd = jnp.where(q_mask, q_ref[0], 0.0)
    q_bdb = q_bd.astype(BF16)

    s = [_dot(q_bdb, k_refs[u][0].astype(BF16)) for u in range(npg)]
    m_old = m_ref[...]
    m_new = m_old
    for u in range(npg):
        m_new = jnp.maximum(m_new, jnp.max(s[u], axis=-1, keepdims=True))
    alpha = jnp.exp(m_old - m_new)
    p = [jnp.exp(s[u] - m_new) for u in range(npg)]
    l_new = alpha * l_ref[...]
    for u in range(npg):
        l_new = l_new + jnp.sum(p[u], axis=-1, keepdims=True)
    p_rep = _dot(jnp.concatenate(p, axis=0).astype(BF16), rep_ref[...])
    acc = alpha * acc_ref[...]
    for u in range(npg):
        p_exp = jnp.where(p_mask, p_rep[u * N_SUB:(u + 1) * N_SUB], 0.0).astype(BF16)
        acc = acc + _dot(p_exp, v_refs[u][0].astype(BF16))
    m_ref[...] = m_new
    l_ref[...] = l_new
    acc_ref[...] = acc

    @pl.when(j == pl.num_programs(1) - 1)
    def _():
        s_new = jnp.sum(q_bd * kn_ref[0], axis=-1, keepdims=True)
        m_fin = jnp.maximum(m_new, s_new)
        a_fin = jnp.exp(m_new - m_fin)
        p_new = jnp.exp(s_new - m_fin)
        l_fin = a_fin * l_new + p_new
        v_new = jnp.concatenate([vn_ref[0], vn_ref[0]], axis=0)
        out = (a_fin * acc + p_new * v_new) / l_fin
        lam = _diff_lambda(lam_ref, lambda_init)
        o = out[0:DIFF_HEADS] - lam * out[DIFF_HEADS:N_SUB]
        ms = jnp.mean(o * o, axis=-1, keepdims=True)
        o_ref[0] = (o * lax.rsqrt(ms + NORM_EPS) * sw_ref[...] * (1.0 - lambda_init)).astype(o_ref.dtype)


def _attn_decode(page_table, k_pages, v_pages, q, k_new, v_new, lam_vecs, subln_w, lambda_init):
    nb, n_pages = page_table.shape
    npg = DEC_PAGES_PER_STEP
    page_blk = (1,) + k_pages.shape[1:]
    assert v_pages.shape[1:] == k_pages.shape[1:]

    def page_spec(u):
        return pl.BlockSpec(page_blk, lambda b, j, pt: (pt[b, j * npg + u], 0, 0))

    per_b = lambda b, j, pt: (b, 0, 0)
    const2 = lambda b, j, pt: (0, 0)
    rep = (jnp.arange(PAGE_SIZE)[:, None] == jnp.arange(PAGE_SIZE * DIFF_HEADS)[None, :] // DIFF_HEADS).astype(BF16)
    grid_spec = pltpu.PrefetchScalarGridSpec(
        num_scalar_prefetch=1,
        grid=(nb, n_pages // npg),
        in_specs=[page_spec(u) for u in range(npg)] + [page_spec(u) for u in range(npg)] + [
            pl.BlockSpec((1, 1, HQK), per_b), pl.BlockSpec((1, 1, HQK), per_b),
            pl.BlockSpec((1, DIFF_HEADS, DIFF_V_DIM), per_b),
            pl.BlockSpec(rep.shape, const2), pl.BlockSpec(lam_vecs.shape, const2),
            pl.BlockSpec(subln_w.shape, const2)],
        out_specs=pl.BlockSpec((1, DIFF_HEADS, DIFF_V_DIM), per_b),
        scratch_shapes=[pltpu.VMEM((N_SUB, 1), F32), pltpu.VMEM((N_SUB, 1), F32),
                        pltpu.VMEM((N_SUB, DIFF_V_DIM), F32)],
    )
    kern = functools.partial(_attn_decode_kernel, lambda_init=lambda_init)
    return pl.pallas_call(
        kern,
        grid_spec=grid_spec,
        out_shape=jax.ShapeDtypeStruct((nb, DIFF_HEADS, DIFF_V_DIM), BF16),
        compiler_params=_cparams(("parallel", "arbitrary")),
        name="attn_decode",
    )(page_table, *([k_pages] * npg), *([v_pages] * npg), q, k_new, v_new, rep, lam_vecs, subln_w)


def kernel(x_prompt, x_sample, cache_k, cache_v, state_ssm, state_conv, page_table, meta_tokens, a_w_in, a_conv_w, a_conv_b, a_dt_bias, a_A_log, a_D, a_norm_w, a_w_out, kv_w_k, kv_w_v, b_w_q, b_lambda, b_subln_w, b_w_o, ffn_w_gate, ffn_w_up, ffn_w_down, ln_mix_w, ln_mix_b, ln_ffn_w, ln_ffn_b):
    bp, seq, _ = x_prompt.shape
    bs = x_sample.shape[0]
    n_pages = page_table.shape[1]
    past_len = n_pages * PAGE_SIZE
    lp = PAD_ROWS + N_META + seq
    n_chunks = lp // CHUNK
    tm_p = 512
    lambda_init = 0.8 - 0.6 * math.exp(-0.3 * 1)
    scale = DIFF_QK_DIM ** -0.5

    w_in = a_w_in[0]
    wz = w_in[:, :D_INNER].astype(BF16)
    wx = w_in[:, D_INNER:D_INNER + CONV_DIM].astype(BF16)
    wdt = jnp.pad(w_in[:, D_INNER + CONV_DIM:], ((0, 0), (0, LANES - SSM_HEADS))).astype(BF16)
    pad_h = lambda v: jnp.pad(v.reshape(1, SSM_HEADS), ((0, 0), (0, LANES - SSM_HEADS)))
    dt_bias, a_log = pad_h(a_dt_bias[0]), pad_h(a_A_log[0])
    conv_w, conv_b = a_conv_w[0], a_conv_b[0].reshape(1, CONV_DIM)
    d_exp = jnp.repeat(a_D[0], SSM_HEAD_DIM).reshape(1, D_INNER)
    norm_w = a_norm_w[0].reshape(1, D_INNER)
    w_out = a_w_out[0].astype(BF16)
    wk, wv = kv_w_k.astype(BF16), kv_w_v.astype(BF16)
    wq = (b_w_q[0] * scale).astype(BF16)
    wo = b_w_o[0].astype(BF16)
    wg, wu, wd = ffn_w_gate.astype(BF16), ffn_w_up.astype(BF16), ffn_w_down.astype(BF16)
    ln = lambda t, l: t[l].reshape(1, D_MODEL)
    subln_w = b_subln_w[0].reshape(1, DIFF_V_DIM)
    lam_vecs = b_lambda[0]

    hp = jnp.concatenate([jnp.zeros((bp, PAD_ROWS, D_MODEL), x_prompt.dtype),
                          jnp.broadcast_to(meta_tokens[None], (bp, N_META, D_MODEL)), x_prompt], axis=1)
    hp = hp.reshape(bp * lp, D_MODEL)
    z, xbc, dt_raw = _in_proj(hp, wz, wx, wdt, tm_p)
    y, conv_tail, ssm_p = _mamba_prompt(xbc, z, dt_raw, conv_w, conv_b, dt_bias, a_log, d_exp, norm_w, bp, n_chunks)
    hp = _proj_ln(y, hp, w_out, ln(ln_mix_w, 0), ln(ln_mix_b, 0), tm_p)
    hp = _ffn_ln(hp, wg[0], wu[0], wd[0], ln(ln_ffn_w, 0), ln(ln_ffn_b, 0), tm_p)
    pos_p = jnp.maximum(jnp.arange(lp) - PAD_ROWS, 0)
    tabs = [jnp.tile(t, (bp, 1)) for t in _rope_tables(pos_p)]
    k_p, v_p, kb_p, vb_p, qb_p = _kvq(hp, wk, wv, wq, *tabs, tm_p, BF16)
    to3 = lambda t: t.reshape(bp, lp, D_MODEL)
    o_p = _attn_prompt(to3(qb_p), to3(kb_p), to3(vb_p), lam_vecs, subln_w, lambda_init)
    hp = _proj_ln(o_p.reshape(bp * lp, D_MODEL), hp, wo, ln(ln_mix_w, 1), ln(ln_mix_b, 1), tm_p)
    hp = _ffn_ln(hp, wg[1], wu[1], wd[1], ln(ln_ffn_w, 1), ln(ln_ffn_b, 1), tm_p)

    y_prompt = to3(hp)[:, CHUNK:]
    k_prompt = to3(k_p)[:, PAD_ROWS:].reshape(bp, N_META + seq, DIFF_HEADS, 2, DIFF_QK_DIM)
    v_prompt = to3(v_p)[:, PAD_ROWS:].reshape(bp, N_META + seq, DIFF_HEADS, DIFF_V_DIM)
    ssm_prompt = ssm_p[None]
    conv_prompt = conv_tail[None, :, SUBLANES - (D_CONV - 1):]

    hs = x_sample.reshape(bs, D_MODEL)
    z_s, xbc_s, dt_raw_s = _in_proj(hs, wz, wx, wdt, bs)
    prev = jnp.transpose(state_conv[0], (1, 0, 2))
    xc_s, conv_s, dt_s, decay_s = _conv_step(xbc_s, prev, conv_w, conv_b, dt_raw_s, dt_bias, a_log)
    ph = lambda t: jnp.swapaxes(t.reshape(-1, SSM_HEADS, SSM_HEAD_DIM), 1, 2)
    y_s, ssm_s = _ssd_step(dt_s[:, :SSM_HEADS], decay_s[:, :SSM_HEADS],
                           ph(xc_s[:, :D_INNER]), ph(z_s),
                           xc_s[:, D_INNER:D_INNER + GN].reshape(bs, SSM_GROUPS, SSM_STATE),
                           xc_s[:, D_INNER + GN:].reshape(bs, SSM_GROUPS, SSM_STATE),
                           state_ssm[0], ph(d_exp)[0], ph(norm_w)[0])
    y_s = jnp.swapaxes(y_s, 1, 2).reshape(bs, D_INNER).astype(BF16)
    hs = _proj_ln(y_s, hs, w_out, ln(ln_mix_w, 0), ln(ln_mix_b, 0), bs)
    hs = _ffn_ln(hs, wg[0], wu[0], wd[0], ln(ln_ffn_w, 0), ln(ln_ffn_b, 0), bs)
    tabs_s = _rope_tables(jnp.full((bs,), past_len, jnp.int32))
    k_s, v_s, _, _, q_s = _kvq(hs, wk, wv, wq, *tabs_s, bs, F32)
    n_pool = cache_k.shape[0]
    k_pages = jnp.transpose(cache_k, (0, 2, 3, 4, 1)).reshape(n_pool, HQK, PAGE_SIZE)
    v_pages = cache_v.reshape(n_pool, PAGE_SIZE * DIFF_HEADS, DIFF_V_DIM)
    o_s = _attn_decode(page_table, k_pages, v_pages, q_s.reshape(bs, 1, HQK), k_s.reshape(bs, 1, HQK),
                       v_s.reshape(bs, DIFF_HEADS, DIFF_V_DIM), lam_vecs, subln_w, lambda_init)
    hs = _proj_ln(o_s.reshape(bs, D_MODEL), hs, wo, ln(ln_mix_w, 1), ln(ln_mix_b, 1), bs)
    hs = _ffn_ln(hs, wg[1], wu[1], wd[1], ln(ln_ffn_w, 1), ln(ln_ffn_b, 1), bs)

    y_sample = hs.reshape(bs, 1, D_MODEL)
    k_sample = k_s.reshape(bs, 1, DIFF_HEADS, 2, DIFF_QK_DIM)
    v_sample = v_s.reshape(bs, 1, DIFF_HEADS, DIFF_V_DIM)
    ssm_sample = ssm_s[None]
    conv_sample = jnp.transpose(conv_s, (1, 0, 2))[None]
    return (y_prompt, y_sample, k_prompt, v_prompt, ssm_prompt, conv_prompt, k_sample, v_sample, ssm_sample,
            conv_sample)
```

```python
import functools
import math

import jax
import jax.numpy as jnp
from jax import lax
from jax.experimental import pallas as pl
from jax.experimental.pallas import tpu as pltpu

F32 = jnp.float32
BF16 = jnp.bfloat16

D_MODEL = 1024
N_META = 16
D_INNER = 2048
SSM_HEADS = 32
SSM_HEAD_DIM = 64
SSM_GROUPS = 4
SSM_STATE = 128
D_CONV = 4
GN = SSM_GROUPS * SSM_STATE
CONV_DIM = D_INNER + 2 * GN
DIFF_HEADS = 8
DIFF_QK_DIM = 64
DIFF_V_DIM = 128
HQK = DIFF_HEADS * 2 * DIFF_QK_DIM
ROT_DIM = DIFF_QK_DIM // 4
ROPE_THETA = 500000.0
D_FF = 2816
DEPTH = 2
DEEPNORM_ALPHA = (2 * DEPTH) ** 0.25
NORM_EPS = 1e-5
PAGE_SIZE = 128

LANES = 128
SUBLANES = 8
VMEM_LIMIT_BYTES = 56 * 1024 * 1024

CHUNK = 128
PAD_ROWS = CHUNK - N_META
ATT_TQ = 512
DEC_PAGES_PER_STEP = 8


def _cparams(sem):
    return pltpu.CompilerParams(dimension_semantics=sem, vmem_limit_bytes=VMEM_LIMIT_BYTES)


def _const_spec(shape):
    nd = len(shape)
    return pl.BlockSpec(shape, lambda *_: (0,) * nd)


def _layer_norm(x, w, b):
    mu = jnp.mean(x, axis=-1, keepdims=True)
    xc = x - mu
    var = jnp.mean(xc * xc, axis=-1, keepdims=True)
    return xc * lax.rsqrt(var + NORM_EPS) * w + b


def _silu(x):
    return x * (1.0 / (1.0 + jnp.exp(-x)))


def _softplus(x):
    return jnp.maximum(x, 0.0) + jnp.log1p(jnp.exp(-jnp.abs(x)))


def _dot(a, b):
    return jnp.dot(a, b, preferred_element_type=F32)


def _dot_nt(a, b):
    return lax.dot_general(a, b, (((1,), (1,)), ((), ())), preferred_element_type=F32)


NCHUNK = 512


def _in_proj_kernel(x_ref, wz_ref, wx_ref, wdt_ref, z_ref, xbc_ref, dt_ref):
    xb = x_ref[...].astype(BF16)
    for n0 in range(0, D_INNER, NCHUNK):
        z_ref[:, n0:n0 + NCHUNK] = _dot(xb, wz_ref[:, n0:n0 + NCHUNK])
    for n0 in range(0, CONV_DIM, NCHUNK):
        xbc_ref[:, n0:n0 + NCHUNK] = _dot(xb, wx_ref[:, n0:n0 + NCHUNK])
    dt_ref[...] = _dot(xb, wdt_ref[...])


def _in_proj(x, wz, wx, wdt, tm):
    m = x.shape[0]
    row = lambda i: (i, 0)
    return pl.pallas_call(
        _in_proj_kernel,
        grid=(m // tm,),
        in_specs=[pl.BlockSpec((tm, D_MODEL), row), _const_spec(wz.shape), _const_spec(wx.shape),
                  _const_spec(wdt.shape)],
        out_specs=[pl.BlockSpec((tm, D_INNER), row), pl.BlockSpec((tm, CONV_DIM), row),
                   pl.BlockSpec((tm, LANES), row)],
        out_shape=[jax.ShapeDtypeStruct((m, D_INNER), F32), jax.ShapeDtypeStruct((m, CONV_DIM), F32),
                   jax.ShapeDtypeStruct((m, LANES), F32)],
        compiler_params=_cparams(("parallel",)),
        name="in_proj",
    )(x, wz, wx, wdt)


def _mamba_prompt_kernel(xbc_ref, z_ref, dtr_ref, tail0_ref, st0_ref, cw_ref, cb_ref, dtb_ref, alog_ref, dexp_ref,
                         nw_ref, y_ref, conv_out_ref, ssm_out_ref, st_out_ref,
                         ext_ref, st_ref, xc_ref, ybuf_ref, *, pad_rows):
    c = pl.program_id(1)
    n_chunks = pl.num_programs(1)

    @pl.when(c == 0)
    def _():
        ext_ref[0:SUBLANES, :] = tail0_ref[...]
        st_ref[...] = st0_ref[...]

    x_raw = xbc_ref[...]
    ext_ref[SUBLANES:SUBLANES + CHUNK, :] = x_raw
    conv = cb_ref[...] + cw_ref[3:4, :] * x_raw
    for j in range(1, D_CONV):
        conv = conv + cw_ref[3 - j:4 - j, :] * ext_ref[pl.ds(SUBLANES - j, CHUNK), :]
    xc_ref[...] = _silu(conv)
    ext_ref[0:SUBLANES, :] = x_raw[CHUNK - SUBLANES:CHUNK, :]

    @pl.when(c == n_chunks - 1)
    def _():
        conv_out_ref[0] = x_raw[CHUNK - SUBLANES:CHUNK, :]

    row = lax.broadcasted_iota(jnp.int32, (CHUNK, LANES), 0)
    col = lax.broadcasted_iota(jnp.int32, (CHUNK, LANES), 1)
    dt = _softplus(dtr_ref[...] + dtb_ref[...])
    if pad_rows:
        dt = jnp.where(jnp.logical_or(c > 0, row >= pad_rows), dt, 0.0)
    a = dt * (-jnp.exp(alog_ref[...]))
    tri = (row >= col).astype(F32)
    cs = jnp.dot(tri, a, preferred_element_type=F32, precision=lax.Precision.HIGHEST)
    cs_t = cs.T
    dt_t = dt.T
    e_cs = jnp.exp(cs)
    causal = row >= col
    lane_lo = col < SSM_HEAD_DIM

    for g in range(SSM_GROUPS):
        b_g = xc_ref[:, D_INNER + g * SSM_STATE:D_INNER + (g + 1) * SSM_STATE]
        c_g = xc_ref[:, D_INNER + GN + g * SSM_STATE:D_INNER + GN + (g + 1) * SSM_STATE]
        c_gb = c_g.astype(BF16)
        b_gb = b_g.astype(BF16)
        cb = _dot_nt(c_gb, b_gb)
        bt = b_g.T
        g0 = g * (D_INNER // SSM_GROUPS)
        y_off_g = _dot(c_gb, st_ref[:, g0:g0 + D_INNER // SSM_GROUPS].astype(BF16))
        for jp in range(D_INNER // SSM_GROUPS // LANES):
            lo = g0 + jp * LANES
            h0 = lo // SSM_HEAD_DIM
            m_parts, mp_parts, dec_parts, ecol_parts = [], [], [], []
            for h in (h0, h0 + 1):
                colv = cs[:, h:h + 1]
                rowv = cs_t[h:h + 1, :]
                dtrow = dt_t[h:h + 1, :]
                last = cs[CHUNK - 1:CHUNK, h:h + 1]
                lmat = jnp.exp(jnp.where(causal, colv - rowv, -jnp.inf))
                m_parts.append((cb * lmat * dtrow).astype(BF16))
                mp_parts.append((bt * (jnp.exp(last - rowv) * dtrow)).astype(BF16))
                dec_parts.append(jnp.exp(last))
                ecol_parts.append(e_cs[:, h:h + 1])
            lhs = jnp.concatenate([jnp.concatenate(m_parts, axis=1),
                                   jnp.concatenate(mp_parts, axis=1)], axis=0)
            x_pair = xc_ref[:, lo:lo + LANES]
            rhs = jnp.concatenate([jnp.where(lane_lo, x_pair, 0.0).astype(BF16),
                                   jnp.where(lane_lo, 0.0, x_pair).astype(BF16)], axis=0)
            res = _dot(lhs, rhs)
            y_diag = res[0:CHUNK]
            d_state = res[CHUNK:2 * CHUNK]
            decay = jnp.where(lane_lo, dec_parts[0], dec_parts[1])
            ecol = jnp.where(lane_lo, ecol_parts[0], ecol_parts[1])
            st_ref[:, lo:lo + LANES] = st_ref[:, lo:lo + LANES] * decay + d_state
            y_pair = y_diag + y_off_g[:, jp * LANES:(jp + 1) * LANES] * ecol + x_pair * dexp_ref[:, lo:lo + LANES]
            ybuf_ref[:, lo:lo + LANES] = y_pair

    gated = ybuf_ref[...] * _silu(z_ref[...])
    ms = jnp.mean(gated * gated, axis=-1, keepdims=True)
    y_ref[...] = (gated * lax.rsqrt(ms + NORM_EPS) * nw_ref[...]).astype(y_ref.dtype)

    @pl.when(c == n_chunks - 1)
    def _():
        st_out_ref[0] = st_ref[...]
        for jp in range(D_INNER // LANES):
            t = st_ref[:, jp * LANES:(jp + 1) * LANES].T
            ssm_out_ref[0, 2 * jp] = t[0:SSM_HEAD_DIM]
            ssm_out_ref[0, 2 * jp + 1] = t[SSM_HEAD_DIM:2 * SSM_HEAD_DIM]


def _mamba_prompt(xbc, z, dt_raw, tail0, st0, conv_w, conv_b, dt_bias, a_log, d_exp, norm_w, nb, n_chunks, pad_rows):
    m = nb * n_chunks * CHUNK
    blk = lambda b, c: (b * n_chunks + c, 0)
    return pl.pallas_call(
        functools.partial(_mamba_prompt_kernel, pad_rows=pad_rows),
        grid=(nb, n_chunks),
        in_specs=[pl.BlockSpec((CHUNK, CONV_DIM), blk), pl.BlockSpec((CHUNK, D_INNER), blk),
                  pl.BlockSpec((CHUNK, LANES), blk), _const_spec(tail0.shape), _const_spec(st0.shape),
                  _const_spec(conv_w.shape), _const_spec(conv_b.shape), _const_spec(dt_bias.shape),
                  _const_spec(a_log.shape), _const_spec(d_exp.shape), _const_spec(norm_w.shape)],
        out_specs=[pl.BlockSpec((CHUNK, D_INNER), blk),
                   pl.BlockSpec((1, SUBLANES, CONV_DIM), lambda b, c: (b, 0, 0)),
                   pl.BlockSpec((1, SSM_HEADS, SSM_HEAD_DIM, SSM_STATE), lambda b, c: (b, 0, 0, 0)),
                   pl.BlockSpec((1, SSM_STATE, D_INNER), lambda b, c: (b, 0, 0))],
        out_shape=[jax.ShapeDtypeStruct((m, D_INNER), BF16),
                   jax.ShapeDtypeStruct((nb, SUBLANES, CONV_DIM), F32),
                   jax.ShapeDtypeStruct((nb, SSM_HEADS, SSM_HEAD_DIM, SSM_STATE), F32),
                   jax.ShapeDtypeStruct((nb, SSM_STATE, D_INNER), F32)],
        scratch_shapes=[pltpu.VMEM((SUBLANES + CHUNK, CONV_DIM), F32),
                        pltpu.VMEM((SSM_STATE, D_INNER), F32),
                        pltpu.VMEM((CHUNK, CONV_DIM), F32),
                        pltpu.VMEM((CHUNK, D_INNER), F32)],
        compiler_params=_cparams(("parallel", "arbitrary")),
        name="mamba_prompt",
    )(xbc, z, dt_raw, tail0, st0, conv_w, conv_b, dt_bias, a_log, d_exp, norm_w)


def _conv_step_kernel(xbc_ref, prev_ref, cw_ref, cb_ref, dtr_ref, dtb_ref, alog_ref,
                      xc_ref, newconv_ref, dt_ref, decay_ref):
    x_raw = xbc_ref[...]
    conv = cb_ref[...] + cw_ref[3:4, :] * x_raw
    for k in range(D_CONV - 1):
        conv = conv + cw_ref[k:k + 1, :] * prev_ref[k]
    xc_ref[...] = _silu(conv)
    newconv_ref[0] = prev_ref[1]
    newconv_ref[1] = prev_ref[2]
    newconv_ref[2] = x_raw
    dt = _softplus(dtr_ref[...] + dtb_ref[...])
    dt_ref[...] = dt
    decay_ref[...] = jnp.exp(dt * (-jnp.exp(alog_ref[...])))


def _conv_step(xbc, prev, conv_w, conv_b, dt_raw, dt_bias, a_log):
    nb = xbc.shape[0]
    return pl.pallas_call(
        _conv_step_kernel,
        out_shape=[jax.ShapeDtypeStruct((nb, CONV_DIM), F32), jax.ShapeDtypeStruct((D_CONV - 1, nb, CONV_DIM), F32),
                   jax.ShapeDtypeStruct((nb, LANES), F32), jax.ShapeDtypeStruct((nb, LANES), F32)],
        compiler_params=pltpu.CompilerParams(vmem_limit_bytes=VMEM_LIMIT_BYTES),
        name="conv_step",
    )(xbc, prev, conv_w, conv_b, dt_raw, dt_bias, a_log)


def _ssd_step_kernel(dt_sm, decay_sm, x_ref, z_ref, b_ref, c_ref, h_ref, dsk_ref, nw_ref, y_ref, hout_ref):
    bi = pl.program_id(0)
    x_t = x_ref[0]
    lane = lax.broadcasted_iota(jnp.int32, x_t.shape, 1)
    y_t = jnp.zeros(x_t.shape, F32)
    for h in range(SSM_HEADS):
        g = h // (SSM_HEADS // SSM_GROUPS)
        dt = dt_sm[bi, h]
        decay = decay_sm[bi, h]
        b_row = b_ref[0, g:g + 1, :]
        c_row = c_ref[0, g:g + 1, :]
        x_col = x_t[:, h:h + 1]
        h0 = h_ref[0, h]
        cb = jnp.sum(c_row * b_row, axis=-1, keepdims=True)
        y_col = decay * jnp.sum(h0 * c_row, axis=-1, keepdims=True) + (cb * dt) * x_col
        hout_ref[0, h] = decay * h0 + (dt * x_col) * b_row
        y_t = jnp.where(lane == h, y_col, y_t)
    y = y_t + x_t * dsk_ref[...]
    gated = y * _silu(z_ref[0])
    ms = jnp.sum(jnp.sum(gated * gated, axis=-1, keepdims=True), axis=0, keepdims=True) / D_INNER
    y_ref[0] = gated * lax.rsqrt(ms + NORM_EPS) * nw_ref[...]


def _ssd_step(dt, decay, x3t, z3t, b3, c3, h0, d_skip_t, norm_w_t):
    nb = x3t.shape[0]
    per_b = lambda b: (b, 0, 0)
    ph = (1, SSM_HEAD_DIM, SSM_HEADS)
    smem = pl.BlockSpec(memory_space=pltpu.SMEM)
    return pl.pallas_call(
        _ssd_step_kernel,
        grid=(nb,),
        in_specs=[smem, smem, pl.BlockSpec(ph, per_b), pl.BlockSpec(ph, per_b),
                  pl.BlockSpec((1, SSM_GROUPS, SSM_STATE), per_b), pl.BlockSpec((1, SSM_GROUPS, SSM_STATE), per_b),
                  pl.BlockSpec((1, SSM_HEADS, SSM_HEAD_DIM, SSM_STATE), lambda b: (b, 0, 0, 0)),
                  _const_spec(d_skip_t.shape), _const_spec(norm_w_t.shape)],
        out_specs=[pl.BlockSpec(ph, per_b),
                   pl.BlockSpec((1, SSM_HEADS, SSM_HEAD_DIM, SSM_STATE), lambda b: (b, 0, 0, 0))],
        out_shape=[jax.ShapeDtypeStruct((nb, SSM_HEAD_DIM, SSM_HEADS), F32),
                   jax.ShapeDtypeStruct((nb, SSM_HEADS, SSM_HEAD_DIM, SSM_STATE), F32)],
        compiler_params=_cparams(("parallel",)),
        name="ssd_step",
    )(dt, decay, x3t, z3t, b3, c3, h0, d_skip_t, norm_w_t)


def _proj_ln_kernel(a_ref, r_ref, w_ref, lw_ref, lb_ref, o_ref):
    mix = _dot(a_ref[...], w_ref[...])
    o_ref[...] = _layer_norm(DEEPNORM_ALPHA * r_ref[...] + mix, lw_ref[...], lb_ref[...])


def _proj_ln(a, resid, w, ln_w, ln_b, tm):
    m, k = a.shape
    row = lambda i: (i, 0)
    return pl.pallas_call(
        _proj_ln_kernel,
        grid=(m // tm,),
        in_specs=[pl.BlockSpec((tm, k), row), pl.BlockSpec((tm, D_MODEL), row), _const_spec(w.shape),
                  _const_spec(ln_w.shape), _const_spec(ln_b.shape)],
        out_specs=pl.BlockSpec((tm, D_MODEL), row),
        out_shape=jax.ShapeDtypeStruct((m, D_MODEL), F32),
        compiler_params=_cparams(("parallel",)),
        name="proj_ln",
    )(a, resid, w, ln_w, ln_b)


FF_CHUNK = 256


def _ffn_ln_kernel(x_ref, wg_ref, wu_ref, wd_ref, lw_ref, lb_ref, o_ref):
    x = x_ref[...]
    xb = x.astype(BF16)
    acc = jnp.zeros(x.shape, F32)
    for f0 in range(0, D_FF, FF_CHUNK):
        gate = _dot(xb, wg_ref[:, f0:f0 + FF_CHUNK])
        up = _dot(xb, wu_ref[:, f0:f0 + FF_CHUNK])
        act = (_silu(gate) * up).astype(BF16)
        acc = acc + _dot(act, wd_ref[f0:f0 + FF_CHUNK, :])
    o_ref[...] = _layer_norm(DEEPNORM_ALPHA * x + acc, lw_ref[...], lb_ref[...])


def _ffn_ln(x, wg, wu, wd, ln_w, ln_b, tm):
    m = x.shape[0]
    row = lambda i: (i, 0)
    return pl.pallas_call(
        _ffn_ln_kernel,
        grid=(m // tm,),
        in_specs=[pl.BlockSpec((tm, D_MODEL), row), _const_spec(wg.shape), _const_spec(wu.shape),
                  _const_spec(wd.shape), _const_spec(ln_w.shape), _const_spec(ln_b.shape)],
        out_specs=pl.BlockSpec((tm, D_MODEL), row),
        out_shape=jax.ShapeDtypeStruct((m, D_MODEL), F32),
        compiler_params=_cparams(("parallel",)),
        name="ffn_ln",
    )(x, wg, wu, wd, ln_w, ln_b)


def _rotary(x, cos_t, sin_a, sin_b):
    half = ROT_DIM // 2
    parts = []
    for j in range(x.shape[1] // LANES):
        xb = x[:, j * LANES:(j + 1) * LANES]
        fwd = pltpu.roll(xb, LANES - half, 1)
        bwd = pltpu.roll(xb, half, 1)
        parts.append(xb * cos_t + fwd * sin_a + bwd * sin_b)
    return jnp.concatenate(parts, axis=1)


def _kvq_kernel(h_ref, wk_ref, wv_ref, wq_ref, cos_ref, sa_ref, sb_ref, k_ref, v_ref, kb_ref, vb_ref, qb_ref):
    hb = h_ref[...].astype(BF16)
    cos_t, sin_a, sin_b = cos_ref[...], sa_ref[...], sb_ref[...]
    k = _rotary(_dot(hb, wk_ref[...]), cos_t, sin_a, sin_b)
    k_ref[...] = k
    kb_ref[...] = k.astype(BF16)
    v = _dot(hb, wv_ref[...])
    v_ref[...] = v
    vb_ref[...] = v.astype(BF16)
    q = _rotary(_dot(hb, wq_ref[...]), cos_t, sin_a, sin_b)
    qb_ref[...] = q.astype(qb_ref.dtype)


def _kvq(h, wk, wv, wq, cos_t, sin_a, sin_b, tm, q_dtype):
    m = h.shape[0]
    tab_blocks = cos_t.shape[0] // tm
    row = lambda i: (i, 0)
    full = pl.BlockSpec((tm, D_MODEL), row)
    tab = pl.BlockSpec((tm, LANES), lambda i: (i % tab_blocks, 0))
    return pl.pallas_call(
        _kvq_kernel,
        grid=(m // tm,),
        in_specs=[full, _const_spec(wk.shape), _const_spec(wv.shape), _const_spec(wq.shape), tab, tab, tab],
        out_specs=[full, full, full, full, full],
        out_shape=[jax.ShapeDtypeStruct((m, D_MODEL), F32), jax.ShapeDtypeStruct((m, D_MODEL), F32),
                   jax.ShapeDtypeStruct((m, D_MODEL), BF16), jax.ShapeDtypeStruct((m, D_MODEL), BF16),
                   jax.ShapeDtypeStruct((m, D_MODEL), q_dtype)],
        compiler_params=_cparams(("parallel",)),
        name="kvq_proj",
    )(h, wk, wv, wq, cos_t, sin_a, sin_b)


def _rope_tables(pos):
    half = ROT_DIM // 2
    inv_freq = ROPE_THETA ** (-jnp.arange(half, dtype=F32) * 2.0 / ROT_DIM)
    ang = pos.astype(F32)[:, None] * inv_freq[None, :]
    cos, sin = jnp.cos(ang), jnp.sin(ang)
    n = pos.shape[0]
    rest = DIFF_QK_DIM - ROT_DIM
    cos_sub = jnp.concatenate([cos, cos, jnp.ones((n, rest), F32)], axis=1)
    sa_sub = jnp.concatenate([-sin, jnp.zeros((n, half + rest), F32)], axis=1)
    sb_sub = jnp.concatenate([jnp.zeros((n, half), F32), sin, jnp.zeros((n, rest), F32)], axis=1)
    rep = lambda t: jnp.concatenate([t, t], axis=1)
    return rep(cos_sub), rep(sa_sub), rep(sb_sub)


def _diff_lambda(lam_ref, lambda_init):
    lv = lam_ref[...]
    s1 = jnp.sum(lv[0:1] * lv[1:2], axis=-1, keepdims=True)
    s2 = jnp.sum(lv[2:3] * lv[3:4], axis=-1, keepdims=True)
    return jnp.exp(s1) - jnp.exp(s2) + lambda_init


def _fold_lanes(x, op):
    r = x[:, 0:LANES]
    for t in range(1, x.shape[1] // LANES):
        r = op(r, x[:, t * LANES:(t + 1) * LANES])
    return r


def _exp_minus(s, m_rep):
    return jnp.concatenate([jnp.exp(s[:, t * LANES:(t + 1) * LANES] - m_rep)
                            for t in range(s.shape[1] // LANES)], axis=1)


def _attn_prompt_kernel(q_ref, k_ref, v_ref, km_ref, vm_ref, lam_ref, sw_ref, o_ref,
                        s_ref, sm_ref, mx_ref, l_ref, acc_ref, *, lambda_init, meta_pad):
    i = pl.program_id(2)
    tq = ATT_TQ
    q = q_ref[0].astype(F32)
    lane = lax.broadcasted_iota(jnp.int32, (tq, LANES), 1)
    q_sub = (jnp.where(lane < DIFF_QK_DIM, q, 0.0).astype(BF16),
             jnp.where(lane < DIFF_QK_DIM, 0.0, q).astype(BF16))

    meta_keep = lax.broadcasted_iota(jnp.int32, (tq, CHUNK), 1) >= meta_pad
    k_meta = km_ref[...]
    for c in range(2):
        s = jnp.where(meta_keep, _dot_nt(q_sub[c], k_meta), -jnp.inf)
        sm_ref[c] = s
        mx_ref[c] = s

    def qk_block(j, keep):
        kb = k_ref[0, pl.ds(pl.multiple_of(j * tq, tq), tq), :]
        for c in range(2):
            s = _dot_nt(q_sub[c], kb)
            if keep is not None:
                s = jnp.where(keep, s, -jnp.inf)
            s_ref[c, j] = s
            mx_ref[c] = jnp.maximum(mx_ref[c], _fold_lanes(s, jnp.maximum))

    def qk_body(j, carry):
        qk_block(j, None)
        return carry

    lax.fori_loop(0, i, qk_body, 0)
    qk_block(i, lax.broadcasted_iota(jnp.int32, (tq, tq), 1) <= lax.broadcasted_iota(jnp.int32, (tq, tq), 0))

    for c in range(2):
        mx_ref[c] = jnp.broadcast_to(jnp.max(mx_ref[c], axis=-1, keepdims=True), (tq, LANES))

    v_meta = vm_ref[...]
    for c in range(2):
        p = jnp.exp(sm_ref[c] - mx_ref[c])
        l_ref[c] = p
        acc_ref[c] = _dot(p.astype(BF16), v_meta)

    def pv_body(j, carry):
        vb = v_ref[0, pl.ds(pl.multiple_of(j * tq, tq), tq), :]
        for c in range(2):
            p = _exp_minus(s_ref[c, j], mx_ref[c])
            l_ref[c] = l_ref[c] + _fold_lanes(p, jnp.add)
            acc_ref[c] = acc_ref[c] + _dot(p.astype(BF16), vb)
        return carry

    lax.fori_loop(0, i + 1, pv_body, 0)

    lam = _diff_lambda(lam_ref, lambda_init)
    l0 = jnp.sum(l_ref[0], axis=-1, keepdims=True)
    l1 = jnp.sum(l_ref[1], axis=-1, keepdims=True)
    o = acc_ref[0] / l0 - lam * (acc_ref[1] / l1)
    ms = jnp.mean(o * o, axis=-1, keepdims=True)
    o = o * lax.rsqrt(ms + NORM_EPS) * sw_ref[...] * (1.0 - lambda_init)
    o_ref[0] = o.astype(o_ref.dtype)


def _attn_prompt(q3, k3, v3, k_meta, v_meta, lam_vecs, subln_w, lambda_init, meta_pad):
    nb, seq_len, _ = q3.shape
    n_q = seq_len // ATT_TQ
    kern = functools.partial(_attn_prompt_kernel, lambda_init=lambda_init, meta_pad=meta_pad)
    q_spec = pl.BlockSpec((1, ATT_TQ, LANES), lambda b, h, i: (b, i, h))
    kv_spec = pl.BlockSpec((1, seq_len, LANES), lambda b, h, i: (b, 0, h))
    meta_spec = pl.BlockSpec((CHUNK, LANES), lambda b, h, i: (0, h))
    return pl.pallas_call(
        kern,
        grid=(nb, DIFF_HEADS, n_q),
        in_specs=[q_spec, kv_spec, kv_spec, meta_spec, meta_spec,
                  _const_spec(lam_vecs.shape), _const_spec(subln_w.shape)],
        out_specs=q_spec,
        out_shape=jax.ShapeDtypeStruct((nb, seq_len, DIFF_HEADS * DIFF_V_DIM), BF16),
        scratch_shapes=[pltpu.VMEM((2, n_q, ATT_TQ, ATT_TQ), F32),
                        pltpu.VMEM((2, ATT_TQ, CHUNK), F32),
                        pltpu.VMEM((2, ATT_TQ, LANES), F32),
                        pltpu.VMEM((2, ATT_TQ, LANES), F32),
                        pltpu.VMEM((2, ATT_TQ, DIFF_V_DIM), F32)],
        compiler_params=_cparams(("parallel", "parallel", "arbitrary")),
        name="attn_prompt",
    )(q3, k3, v3, k_meta, v_meta, lam_vecs, subln_w)


N_SUB = 2 * DIFF_HEADS


def _attn_decode_kernel(pt_ref, *refs, lambda_init):
    npg = DEC_PAGES_PER_STEP
    k_refs = refs[0:npg]
    v_refs = refs[npg:2 * npg]
    q_ref, kn_ref, vn_ref, rep_ref, lam_ref, sw_ref = refs[2 * npg:2 * npg + 6]
    o_ref = refs[2 * npg + 6]
    m_ref, l_ref, acc_ref = refs[2 * npg + 7:]
    j = pl.program_id(1)

    @pl.when(j == 0)
    def _():
        m_ref[...] = jnp.full(m_ref.shape, -jnp.inf, F32)
        l_ref[...] = jnp.zeros(l_ref.shape, F32)
        acc_ref[...] = jnp.zeros(acc_ref.shape, F32)

    row = lax.broadcasted_iota(jnp.int32, (N_SUB, HQK), 0)
    col = lax.broadcasted_iota(jnp.int32, (N_SUB, HQK), 1)
    row_head = row % DIFF_HEADS
    q_mask = (col // DIFF_QK_DIM) == (2 * row_head + row // DIFF_HEADS)
    p_mask = (col % DIFF_HEADS) == row_head
    q_bd = jnp.where(q_mask, q_ref[0], 0.0)
    q_bdb = q_bd.astype(BF16)

    s = [_dot(q_bdb, k_refs[u][0].astype(BF16)) for u in range(npg)]
    m_old = m_ref[...]
    m_new = m_old
    for u in range(npg):
        m_new = jnp.maximum(m_new, jnp.max(s[u], axis=-1, keepdims=True))
    alpha = jnp.exp(m_old - m_new)
    p = [jnp.exp(s[u] - m_new) for u in range(npg)]
    l_new = alpha * l_ref[...]
    for u in range(npg):
        l_new = l_new + jnp.sum(p[u], axis=-1, keepdims=True)
    p_rep = _dot(jnp.concatenate(p, axis=0).astype(BF16), rep_ref[...])
    acc = alpha * acc_ref[...]
    for u in range(npg):
        p_exp = jnp.where(p_mask, p_rep[u * N_SUB:(u + 1) * N_SUB], 0.0).astype(BF16)
        acc = acc + _dot(p_exp, v_refs[u][0].astype(BF16))
    m_ref[...] = m_new
    l_ref[...] = l_new
    acc_ref[...] = acc

    @pl.when(j == pl.num_programs(1) - 1)
    def _():
        s_new = jnp.sum(q_bd * kn_ref[0], axis=-1, keepdims=True)
        m_fin = jnp.maximum(m_new, s_new)
        a_fin = jnp.exp(m_new - m_fin)
        p_new = jnp.exp(s_new - m_fin)
        l_fin = a_fin * l_new + p_new
        v_new = jnp.concatenate([vn_ref[0], vn_ref[0]], axis=0)
        out = (a_fin * acc + p_new * v_new) / l_fin
        lam = _diff_lambda(lam_ref, lambda_init)
        o = out[0:DIFF_HEADS] - lam * out[DIFF_HEADS:N_SUB]
        ms = jnp.mean(o * o, axis=-1, keepdims=True)
        o_ref[0] = (o * lax.rsqrt(ms + NORM_EPS) * sw_ref[...] * (1.0 - lambda_init)).astype(o_ref.dtype)


def _attn_decode(page_table, k_pages, v_pages, q, k_new, v_new, lam_vecs, subln_w, lambda_init):
    nb, n_pages = page_table.shape
    npg = DEC_PAGES_PER_STEP
    page_blk = (1,) + k_pages.shape[1:]
    assert v_pages.shape[1:] == k_pages.shape[1:]

    def page_spec(u):
        return pl.BlockSpec(page_blk, lambda b, j, pt: (pt[b, j * npg + u], 0, 0))

    per_b = lambda b, j, pt: (b, 0, 0)
    const2 = lambda b, j, pt: (0, 0)
    rep = (jnp.arange(PAGE_SIZE)[:, None] == jnp.arange(PAGE_SIZE * DIFF_HEADS)[None, :] // DIFF_HEADS).astype(BF16)
    grid_spec = pltpu.PrefetchScalarGridSpec(
        num_scalar_prefetch=1,
        grid=(nb, n_pages // npg),
        in_specs=[page_spec(u) for u in range(npg)] + [page_spec(u) for u in range(npg)] + [
            pl.BlockSpec((1, 1, HQK), per_b), pl.BlockSpec((1, 1, HQK), per_b),
            pl.BlockSpec((1, DIFF_HEADS, DIFF_V_DIM), per_b),
            pl.BlockSpec(rep.shape, const2), pl.BlockSpec(lam_vecs.shape, const2),
            pl.BlockSpec(subln_w.shape, const2)],
        out_specs=pl.BlockSpec((1, DIFF_HEADS, DIFF_V_DIM), per_b),
        scratch_shapes=[pltpu.VMEM((N_SUB, 1), F32), pltpu.VMEM((N_SUB, 1), F32),
                        pltpu.VMEM((N_SUB, DIFF_V_DIM), F32)],
    )
    kern = functools.partial(_attn_decode_kernel, lambda_init=lambda_init)
    return pl.pallas_call(
        kern,
        grid_spec=grid_spec,
        out_shape=jax.ShapeDtypeStruct((nb, DIFF_HEADS, DIFF_V_DIM), BF16),
        compiler_params=_cparams(("parallel", "arbitrary")),
        name="attn_decode",
    )(page_table, *([k_pages] * npg), *([v_pages] * npg), q, k_new, v_new, rep, lam_vecs, subln_w)


def kernel(x_prompt, x_sample, cache_k, cache_v, state_ssm, state_conv, page_table, meta_tokens, a_w_in, a_conv_w, a_conv_b, a_dt_bias, a_A_log, a_D, a_norm_w, a_w_out, kv_w_k, kv_w_v, b_w_q, b_lambda, b_subln_w, b_w_o, ffn_w_gate, ffn_w_up, ffn_w_down, ln_mix_w, ln_mix_b, ln_ffn_w, ln_ffn_b):
    bp, seq, _ = x_prompt.shape
    bs = x_sample.shape[0]
    n_pages = page_table.shape[1]
    past_len = n_pages * PAGE_SIZE
    tm_p = 512
    lambda_init = 0.8 - 0.6 * math.exp(-0.3 * 1)
    scale = DIFF_QK_DIM ** -0.5

    w_in = a_w_in[0]
    wz = w_in[:, :D_INNER].astype(BF16)
    wx = w_in[:, D_INNER:D_INNER + CONV_DIM].astype(BF16)
    wdt = jnp.pad(w_in[:, D_INNER + CONV_DIM:], ((0, 0), (0, LANES - SSM_HEADS))).astype(BF16)
    pad_h = lambda v: jnp.pad(v.reshape(1, SSM_HEADS), ((0, 0), (0, LANES - SSM_HEADS)))
    dt_bias, a_log = pad_h(a_dt_bias[0]), pad_h(a_A_log[0])
    conv_w, conv_b = a_conv_w[0], a_conv_b[0].reshape(1, CONV_DIM)
    d_exp = jnp.repeat(a_D[0], SSM_HEAD_DIM).reshape(1, D_INNER)
    norm_w = a_norm_w[0].reshape(1, D_INNER)
    w_out = a_w_out[0].astype(BF16)
    wk, wv = kv_w_k.astype(BF16), kv_w_v.astype(BF16)
    wq = (b_w_q[0] * scale).astype(BF16)
    wo = b_w_o[0].astype(BF16)
    wg, wu, wd = ffn_w_gate.astype(BF16), ffn_w_up.astype(BF16), ffn_w_down.astype(BF16)
    ln = lambda t, l: t[l].reshape(1, D_MODEL)
    subln_w = b_subln_w[0].reshape(1, DIFF_V_DIM)
    lam_vecs = b_lambda[0]

    n_main = bp * seq
    n_aux = CHUNK + bs
    hm = x_prompt.reshape(n_main, D_MODEL)
    ha = jnp.concatenate([jnp.zeros((PAD_ROWS, D_MODEL), F32), meta_tokens, x_sample.reshape(bs, D_MODEL)], axis=0)
    smp = lambda t: t[CHUNK:]

    z_a, xbc_a, dtr_a = _in_proj(ha, wz, wx, wdt, n_aux)
    z_m, xbc_m, dtr_m = _in_proj(hm, wz, wx, wdt, tm_p)
    y_meta, tail_meta, _, st_meta = _mamba_prompt(
        xbc_a, z_a, dtr_a, jnp.zeros((SUBLANES, CONV_DIM), F32), jnp.zeros((SSM_STATE, D_INNER), F32),
        conv_w, conv_b, dt_bias, a_log, d_exp, norm_w, 1, 1, PAD_ROWS)
    y_m, conv_tail, ssm_p, _ = _mamba_prompt(xbc_m, z_m, dtr_m, tail_meta[0], st_meta[0],
                                             conv_w, conv_b, dt_bias, a_log, d_exp, norm_w, bp, seq // CHUNK, 0)
    prev = jnp.transpose(state_conv[0], (1, 0, 2))
    xc_s, conv_s, dt_s, decay_s = _conv_step(smp(xbc_a), prev, conv_w, conv_b, smp(dtr_a), dt_bias, a_log)
    ph = lambda t: jnp.swapaxes(t.reshape(-1, SSM_HEADS, SSM_HEAD_DIM), 1, 2)
    y_s, ssm_s = _ssd_step(dt_s[:, :SSM_HEADS], decay_s[:, :SSM_HEADS],
                           ph(xc_s[:, :D_INNER]), ph(smp(z_a)),
                           xc_s[:, D_INNER:D_INNER + GN].reshape(bs, SSM_GROUPS, SSM_STATE),
                           xc_s[:, D_INNER + GN:].reshape(bs, SSM_GROUPS, SSM_STATE),
                           state_ssm[0], ph(d_exp)[0], ph(norm_w)[0])
    y_a = jnp.concatenate([y_meta, jnp.swapaxes(y_s, 1, 2).reshape(bs, D_INNER).astype(BF16)], axis=0)

    ha = _proj_ln(y_a, ha, w_out, ln(ln_mix_w, 0), ln(ln_mix_b, 0), n_aux)
    ha = _ffn_ln(ha, wg[0], wu[0], wd[0], ln(ln_ffn_w, 0), ln(ln_ffn_b, 0), n_aux)
    hm = _proj_ln(y_m, hm, w_out, ln(ln_mix_w, 0), ln(ln_mix_b, 0), tm_p)
    hm = _ffn_ln(hm, wg[0], wu[0], wd[0], ln(ln_ffn_w, 0), ln(ln_ffn_b, 0), tm_p)
    pos_a = jnp.concatenate([jnp.zeros((PAD_ROWS,), jnp.int32), jnp.arange(N_META),
                             jnp.full((bs,), past_len, jnp.int32)])
    k_a, v_a, kb_a, vb_a, q_a = _kvq(ha, wk, wv, wq, *_rope_tables(pos_a), n_aux, F32)
    k_m, v_m, kb_m, vb_m, qb_m = _kvq(hm, wk, wv, wq, *_rope_tables(N_META + jnp.arange(seq)), tm_p, BF16)

    to3 = lambda t: t.reshape(bp, seq, D_MODEL)
    o_m = _attn_prompt(to3(qb_m), to3(kb_m), to3(vb_m), kb_a, vb_a, lam_vecs, subln_w, lambda_init, PAD_ROWS)
    hm = _proj_ln(o_m.reshape(n_main, D_MODEL), hm, wo, ln(ln_mix_w, 1), ln(ln_mix_b, 1), tm_p)
    hm = _ffn_ln(hm, wg[1], wu[1], wd[1], ln(ln_ffn_w, 1), ln(ln_ffn_b, 1), tm_p)

    hs, k_s, v_s, q_s = smp(ha), smp(k_a), smp(v_a), smp(q_a)
    n_pool = cache_k.shape[0]
    k_pages = jnp.transpose(cache_k, (0, 2, 3, 4, 1)).reshape(n_pool, HQK, PAGE_SIZE)
    v_pages = cache_v.reshape(n_pool, PAGE_SIZE * DIFF_HEADS, DIFF_V_DIM)
    o_s = _attn_decode(page_table, k_pages, v_pages, q_s.reshape(bs, 1, HQK), k_s.reshape(bs, 1, HQK),
                       v_s.reshape(bs, DIFF_HEADS, DIFF_V_DIM), lam_vecs, subln_w, lambda_init)
    hs = _proj_ln(o_s.reshape(bs, D_MODEL), hs, wo, ln(ln_mix_w, 1), ln(ln_mix_b, 1), bs)
    hs = _ffn_ln(hs, wg[1], wu[1], wd[1], ln(ln_ffn_w, 1), ln(ln_ffn_b, 1), bs)

    with_meta = lambda meta, main: jnp.concatenate(
        [jnp.broadcast_to(meta[None, PAD_ROWS:CHUNK], (bp, N_META, D_MODEL)), to3(main)], axis=1)
    y_prompt = to3(hm)
    k_prompt = with_meta(k_a, k_m).reshape(bp, N_META + seq, DIFF_HEADS, 2, DIFF_QK_DIM)
    v_prompt = with_meta(v_a, v_m).reshape(bp, N_META + seq, DIFF_HEADS, DIFF_V_DIM)
    ssm_prompt = ssm_p[None]
    conv_prompt = conv_tail[None, :, SUBLANES - (D_CONV - 1):]
    y_sample = hs.reshape(bs, 1, D_MODEL)
    k_sample = k_s.reshape(bs, 1, DIFF_HEADS, 2, DIFF_QK_DIM)
    v_sample = v_s.reshape(bs, 1, DIFF_HEADS, DIFF_V_DIM)
    ssm_sample = ssm_s[None]
    conv_sample = jnp.transpose(conv_s, (1, 0, 2))[None]
    return (y_prompt, y_sample, k_prompt, v_prompt, ssm_prompt, conv_prompt, k_sample, v_sample, ssm_sample,
            conv_sample)
```

```python
import functools
import math

import jax
import jax.numpy as jnp
from jax import lax
from jax.experimental import pallas as pl
from jax.experimental.pallas import tpu as pltpu

F32 = jnp.float32
BF16 = jnp.bfloat16

D_MODEL = 1024
N_META = 16
D_INNER = 2048
SSM_HEADS = 32
SSM_HEAD_DIM = 64
SSM_GROUPS = 4
SSM_STATE = 128
D_CONV = 4
GN = SSM_GROUPS * SSM_STATE
CONV_DIM = D_INNER + 2 * GN
DIFF_HEADS = 8
DIFF_QK_DIM = 64
DIFF_V_DIM = 128
HQK = DIFF_HEADS * 2 * DIFF_QK_DIM
ROT_DIM = DIFF_QK_DIM // 4
ROPE_THETA = 500000.0
D_FF = 2816
DEPTH = 2
DEEPNORM_ALPHA = (2 * DEPTH) ** 0.25
NORM_EPS = 1e-5
PAGE_SIZE = 128
LOG2_E = math.log2(math.e)

LANES = 128
SUBLANES = 8
VMEM_LIMIT_BYTES = 56 * 1024 * 1024

CHUNK = 128
CONV_COLS = 512
PAD_ROWS = CHUNK - N_META
ATT_TQ = 512
ATT_HEADS = 2
DEC_PAGES_PER_STEP = 8
ROW_TILE = 512


def _cparams(sem):
    return pltpu.CompilerParams(dimension_semantics=sem, vmem_limit_bytes=VMEM_LIMIT_BYTES)


def _const_spec(shape):
    nd = len(shape)
    return pl.BlockSpec(shape, lambda *_: (0,) * nd)


def _weight_spec(shape):
    nd = len(shape)
    return pl.BlockSpec(shape, lambda *_: (0,) * nd, pipeline_mode=pl.Buffered(1))


def _ROW(i, *_):
    return (i, 0)


def _layer_norm(x, w, b):
    mu = jnp.mean(x, axis=-1, keepdims=True)
    xc = x - mu
    var = jnp.mean(xc * xc, axis=-1, keepdims=True)
    return xc * lax.rsqrt(var + NORM_EPS) * w + b


def _silu(x):
    h = 0.5 * x
    return h + h * jnp.tanh(h)


def _softplus(x):
    return jnp.maximum(x, 0.0) + jnp.log1p(jnp.exp(-jnp.abs(x)))


def _dot(a, b):
    return jnp.dot(a, b, preferred_element_type=F32)


def _dot_nt(a, b):
    return lax.dot_general(a, b, (((1,), (1,)), ((), ())), preferred_element_type=F32)


NCHUNK = 512


def _in_proj_kernel(x_ref, wz_ref, wx_ref, wdt_ref, z_ref, xbc_ref, dt_ref):
    xb = x_ref[...].astype(BF16)
    for n0 in range(0, D_INNER, NCHUNK):
        z_ref[:, n0:n0 + NCHUNK] = _dot(xb, wz_ref[:, n0:n0 + NCHUNK])
    for n0 in range(0, CONV_DIM, NCHUNK):
        xbc_ref[:, n0:n0 + NCHUNK] = _dot(xb, wx_ref[:, n0:n0 + NCHUNK])
    dt_ref[...] = _dot(xb, wdt_ref[...])


def _in_proj(x, wz, wx, wdt, tm, dec=None):
    m = x.shape[0]
    return _row_tiled_call(
        _in_proj_kernel, "in_proj", (x, wz, wx, wdt),
        in_specs=[pl.BlockSpec((tm, D_MODEL), _ROW), _weight_spec(wz.shape), _weight_spec(wx.shape),
                  _weight_spec(wdt.shape)],
        out_specs=[pl.BlockSpec((tm, D_INNER), _ROW), pl.BlockSpec((tm, CONV_DIM), _ROW),
                   pl.BlockSpec((tm, LANES), _ROW)],
        out_shape=[jax.ShapeDtypeStruct((m, D_INNER), F32), jax.ShapeDtypeStruct((m, CONV_DIM), F32),
                   jax.ShapeDtypeStruct((m, LANES), F32)],
        n_steps=m // tm, dec=dec)


def _mamba_prompt_kernel(xbc_ref, z_ref, dtr_ref, tail0_ref, st0_ref, cw_ref, cb_ref, dtb_ref, alog_ref, dexp_ref,
                         nw_ref, y_ref, conv_out_ref, ssm_out_ref, st_out_ref,
                         ext_ref, st_ref, xc_ref, ybuf_ref, *, pad_rows):
    c = pl.program_id(1)
    n_chunks = pl.num_programs(1)

    @pl.when(c == 0)
    def _():
        ext_ref[0:SUBLANES, :] = tail0_ref[...]
        st_ref[...] = st0_ref[...]

    x_raw = xbc_ref[...]
    ext_ref[SUBLANES:SUBLANES + CHUNK, :] = x_raw
    for n0 in range(0, CONV_DIM, CONV_COLS):
        cols = slice(n0, n0 + CONV_COLS)
        ext = ext_ref[:, cols]
        conv = cb_ref[:, cols] + cw_ref[3:4, cols] * ext[SUBLANES:]
        for j in range(1, D_CONV):
            conv = conv + cw_ref[3 - j:4 - j, cols] * pltpu.roll(ext, j, 0)[SUBLANES:]
        xc_ref[:, cols] = _silu(conv)
    ext_ref[0:SUBLANES, :] = x_raw[CHUNK - SUBLANES:CHUNK, :]

    @pl.when(c == n_chunks - 1)
    def _():
        conv_out_ref[0] = x_raw[CHUNK - SUBLANES:CHUNK, :]

    row = lax.broadcasted_iota(jnp.int32, (CHUNK, LANES), 0)
    col = lax.broadcasted_iota(jnp.int32, (CHUNK, LANES), 1)
    dt = _softplus(dtr_ref[...] + dtb_ref[...])
    if pad_rows:
        dt = jnp.where(jnp.logical_or(c > 0, row >= pad_rows), dt, 0.0)
    a = dt * (-jnp.exp(alog_ref[...]))
    tri = (row >= col).astype(F32)
    cs = jnp.dot(tri, a, preferred_element_type=F32, precision=lax.Precision.HIGHEST)
    cs_t = cs.T
    dt_t = dt.T
    e_cs = jnp.exp(cs)
    causal = row >= col
    lane_lo = col < SSM_HEAD_DIM

    for g in range(SSM_GROUPS):
        b_g = xc_ref[:, D_INNER + g * SSM_STATE:D_INNER + (g + 1) * SSM_STATE]
        c_g = xc_ref[:, D_INNER + GN + g * SSM_STATE:D_INNER + GN + (g + 1) * SSM_STATE]
        c_gb = c_g.astype(BF16)
        b_gb = b_g.astype(BF16)
        cb = _dot_nt(c_gb, b_gb)
        bt = b_g.T
        g0 = g * (D_INNER // SSM_GROUPS)
        y_off_g = _dot(c_gb, st_ref[:, g0:g0 + D_INNER // SSM_GROUPS].astype(BF16))
        for jp in range(D_INNER // SSM_GROUPS // LANES):
            lo = g0 + jp * LANES
            h0 = lo // SSM_HEAD_DIM
            m_parts, mp_parts, dec_parts, ecol_parts = [], [], [], []
            for h in (h0, h0 + 1):
                colv = cs[:, h:h + 1]
                rowv = cs_t[h:h + 1, :]
                dtrow = dt_t[h:h + 1, :]
                last = cs[CHUNK - 1:CHUNK, h:h + 1]
                lmat = jnp.exp(jnp.where(causal, colv - rowv, -jnp.inf))
                m_parts.append((cb * lmat * dtrow).astype(BF16))
                mp_parts.append((bt * (jnp.exp(last - rowv) * dtrow)).astype(BF16))
                dec_parts.append(jnp.exp(last))
                ecol_parts.append(e_cs[:, h:h + 1])
            lhs = jnp.concatenate([jnp.concatenate(m_parts, axis=1),
                                   jnp.concatenate(mp_parts, axis=1)], axis=0)
            x_pair = xc_ref[:, lo:lo + LANES]
            rhs = jnp.concatenate([jnp.where(lane_lo, x_pair, 0.0).astype(BF16),
                                   jnp.where(lane_lo, 0.0, x_pair).astype(BF16)], axis=0)
            res = _dot(lhs, rhs)
            y_diag = res[0:CHUNK]
            d_state = res[CHUNK:2 * CHUNK]
            decay = jnp.where(lane_lo, dec_parts[0], dec_parts[1])
            ecol = jnp.where(lane_lo, ecol_parts[0], ecol_parts[1])
            st_ref[:, lo:lo + LANES] = st_ref[:, lo:lo + LANES] * decay + d_state
            y_pair = y_diag + y_off_g[:, jp * LANES:(jp + 1) * LANES] * ecol + x_pair * dexp_ref[:, lo:lo + LANES]
            ybuf_ref[:, lo:lo + LANES] = y_pair

    gated = ybuf_ref[...] * _silu(z_ref[...])
    ms = jnp.mean(gated * gated, axis=-1, keepdims=True)
    y_ref[...] = (gated * lax.rsqrt(ms + NORM_EPS) * nw_ref[...]).astype(y_ref.dtype)

    @pl.when(c == n_chunks - 1)
    def _():
        st_out_ref[0] = st_ref[...]
        for jp in range(D_INNER // LANES):
            t = st_ref[:, jp * LANES:(jp + 1) * LANES].T
            ssm_out_ref[0, 2 * jp] = t[0:SSM_HEAD_DIM]
            ssm_out_ref[0, 2 * jp + 1] = t[SSM_HEAD_DIM:2 * SSM_HEAD_DIM]


def _mamba_prompt(xbc, z, dt_raw, tail0, st0, conv_w, conv_b, dt_bias, a_log, d_exp, norm_w, nb, n_chunks, pad_rows):
    m = nb * n_chunks * CHUNK
    blk = lambda b, c: (b * n_chunks + c, 0)
    return pl.pallas_call(
        functools.partial(_mamba_prompt_kernel, pad_rows=pad_rows),
        grid=(nb, n_chunks),
        in_specs=[pl.BlockSpec((CHUNK, CONV_DIM), blk), pl.BlockSpec((CHUNK, D_INNER), blk),
                  pl.BlockSpec((CHUNK, LANES), blk), _const_spec(tail0.shape), _const_spec(st0.shape),
                  _const_spec(conv_w.shape), _const_spec(conv_b.shape), _const_spec(dt_bias.shape),
                  _const_spec(a_log.shape), _const_spec(d_exp.shape), _const_spec(norm_w.shape)],
        out_specs=[pl.BlockSpec((CHUNK, D_INNER), blk),
                   pl.BlockSpec((1, SUBLANES, CONV_DIM), lambda b, c: (b, 0, 0)),
                   pl.BlockSpec((1, SSM_HEADS, SSM_HEAD_DIM, SSM_STATE), lambda b, c: (b, 0, 0, 0)),
                   pl.BlockSpec((1, SSM_STATE, D_INNER), lambda b, c: (b, 0, 0))],
        out_shape=[jax.ShapeDtypeStruct((m, D_INNER), BF16),
                   jax.ShapeDtypeStruct((nb, SUBLANES, CONV_DIM), F32),
                   jax.ShapeDtypeStruct((nb, SSM_HEADS, SSM_HEAD_DIM, SSM_STATE), F32),
                   jax.ShapeDtypeStruct((nb, SSM_STATE, D_INNER), F32)],
        scratch_shapes=[pltpu.VMEM((SUBLANES + CHUNK, CONV_DIM), F32),
                        pltpu.VMEM((SSM_STATE, D_INNER), F32),
                        pltpu.VMEM((CHUNK, CONV_DIM), F32),
                        pltpu.VMEM((CHUNK, D_INNER), F32)],
        compiler_params=_cparams(("parallel", "arbitrary")),
        name="mamba_prompt",
    )(xbc, z, dt_raw, tail0, st0, conv_w, conv_b, dt_bias, a_log, d_exp, norm_w)


def _conv_step_kernel(xbc_ref, prev_ref, cw_ref, cb_ref, dtr_ref, dtb_ref, alog_ref,
                      xc_ref, newconv_ref, dt_ref, decay_ref):
    x_raw = xbc_ref[...]
    conv = cb_ref[...] + cw_ref[3:4, :] * x_raw
    for k in range(D_CONV - 1):
        conv = conv + cw_ref[k:k + 1, :] * prev_ref[k]
    xc_ref[...] = _silu(conv)
    newconv_ref[0] = prev_ref[1]
    newconv_ref[1] = prev_ref[2]
    newconv_ref[2] = x_raw
    dt = _softplus(dtr_ref[...] + dtb_ref[...])
    dt_ref[...] = dt
    decay_ref[...] = jnp.exp(dt * (-jnp.exp(alog_ref[...])))


def _conv_step(xbc, prev, conv_w, conv_b, dt_raw, dt_bias, a_log):
    nb = xbc.shape[0]
    return pl.pallas_call(
        _conv_step_kernel,
        out_shape=[jax.ShapeDtypeStruct((nb, CONV_DIM), F32), jax.ShapeDtypeStruct((D_CONV - 1, nb, CONV_DIM), F32),
                   jax.ShapeDtypeStruct((nb, LANES), F32), jax.ShapeDtypeStruct((nb, LANES), F32)],
        compiler_params=pltpu.CompilerParams(vmem_limit_bytes=VMEM_LIMIT_BYTES),
        name="conv_step",
    )(xbc, prev, conv_w, conv_b, dt_raw, dt_bias, a_log)


N_PAIRS = D_INNER // LANES


def _ssd_step_kernel(x_ref, z_ref, dt_ref, dec_ref, b_ref, c_ref, h_ref, dsk_ref, nw_ref, y_ref, hout_ref):
    x = x_ref[0]
    pad = jnp.zeros((LANES - N_PAIRS, LANES), F32)
    cols = lambda v: jnp.concatenate([v, pad], axis=0).T
    x_t, dtx_t, dec_t = cols(x), cols(dt_ref[0] * x), cols(dec_ref[0])
    lane = lax.broadcasted_iota(jnp.int32, (LANES, LANES), 1)
    y_t = jnp.zeros((LANES, LANES), F32)
    for r in range(N_PAIRS):
        g = r // (N_PAIRS // SSM_GROUPS)
        b_row = b_ref[0, g:g + 1, :]
        c_row = c_ref[0, g:g + 1, :]
        dtx_col, dec_col = dtx_t[:, r:r + 1], dec_t[:, r:r + 1]
        h0 = h_ref[0, r]
        cb = jnp.sum(c_row * b_row, axis=-1, keepdims=True)
        y_col = dec_col * jnp.sum(h0 * c_row, axis=-1, keepdims=True) + cb * dtx_col
        hout_ref[0, r] = dec_col * h0 + dtx_col * b_row
        y_t = jnp.where(lane == r, y_col, y_t)
    y = y_t.T[0:N_PAIRS] + x * dsk_ref[...]
    gated = y * _silu(z_ref[0])
    ms = jnp.sum(jnp.sum(gated * gated, axis=-1, keepdims=True), axis=0, keepdims=True) / D_INNER
    y_ref[0] = (gated * lax.rsqrt(ms + NORM_EPS) * nw_ref[...]).astype(y_ref.dtype)


def _ssd_step(x, z, dt_ch, decay_ch, b3, c3, h0, d_skip, norm_w):
    nb = x.shape[0]
    per_b = lambda b: (b, 0, 0)
    vec = pl.BlockSpec((1, N_PAIRS, LANES), per_b)
    pairs = lambda t: t.reshape(-1, N_PAIRS, LANES)
    st_spec = pl.BlockSpec((1, N_PAIRS, LANES, SSM_STATE), lambda b: (b, 0, 0, 0))
    grp = pl.BlockSpec((1, SSM_GROUPS, SSM_STATE), per_b)
    y, h_new = pl.pallas_call(
        _ssd_step_kernel,
        grid=(nb,),
        in_specs=[vec, vec, vec, vec, grp, grp, st_spec, _const_spec((N_PAIRS, LANES)), _const_spec((N_PAIRS, LANES))],
        out_specs=[vec, st_spec],
        out_shape=[jax.ShapeDtypeStruct((nb, N_PAIRS, LANES), BF16),
                   jax.ShapeDtypeStruct((nb, N_PAIRS, LANES, SSM_STATE), F32)],
        compiler_params=_cparams(("parallel",)),
        name="ssd_step",
    )(pairs(x), pairs(z), pairs(dt_ch), pairs(decay_ch), b3, c3, h0.reshape(nb, N_PAIRS, LANES, SSM_STATE),
      d_skip.reshape(N_PAIRS, LANES), norm_w.reshape(N_PAIRS, LANES))
    return y.reshape(nb, D_INNER), h_new.reshape(h0.shape)


FF_CHUNK = 256


def _proj_ffn_ln_kernel(a_ref, r_ref, w_ref, mw_ref, mb_ref, wg_ref, wu_ref, wd_ref, fw_ref, fb_ref, o_ref):
    mix = _dot(a_ref[...], w_ref[...])
    x = _layer_norm(DEEPNORM_ALPHA * r_ref[...] + mix, mw_ref[...], mb_ref[...])
    xb = x.astype(BF16)
    acc = jnp.zeros(x.shape, F32)
    for f0 in range(0, D_FF, FF_CHUNK):
        gate = _dot(xb, wg_ref[:, f0:f0 + FF_CHUNK])
        up = _dot(xb, wu_ref[:, f0:f0 + FF_CHUNK])
        act = (_silu(gate) * up).astype(BF16)
        acc = acc + _dot(act, wd_ref[f0:f0 + FF_CHUNK, :])
    o_ref[...] = _layer_norm(DEEPNORM_ALPHA * x + acc, fw_ref[...], fb_ref[...])


def _proj_ffn_ln(a, resid, w, mix_w, mix_b, wg, wu, wd, ffn_w, ffn_b, tm):
    m, k = a.shape
    vec = _const_spec((1, D_MODEL))
    return _row_tiled_call(
        _proj_ffn_ln_kernel, "proj_ffn_ln", (a, resid, w, mix_w, mix_b, wg, wu, wd, ffn_w, ffn_b),
        in_specs=[pl.BlockSpec((tm, k), _ROW), pl.BlockSpec((tm, D_MODEL), _ROW), _weight_spec(w.shape), vec, vec,
                  _weight_spec(wg.shape), _weight_spec(wu.shape), _weight_spec(wd.shape), vec, vec],
        out_specs=[pl.BlockSpec((tm, D_MODEL), _ROW)],
        out_shape=[jax.ShapeDtypeStruct((m, D_MODEL), F32)],
        n_steps=m // tm)[0]


def _rotary(x, cos_t, sin_a, sin_b):
    half = ROT_DIM // 2
    parts = []
    for j in range(x.shape[1] // LANES):
        xb = x[:, j * LANES:(j + 1) * LANES]
        fwd = pltpu.roll(xb, LANES - half, 1)
        bwd = pltpu.roll(xb, half, 1)
        parts.append(xb * cos_t + fwd * sin_a + bwd * sin_b)
    return jnp.concatenate(parts, axis=1)


def _kvq_kernel(h_ref, wk_ref, wv_ref, wq_ref, cos_ref, sa_ref, sb_ref, k_ref, v_ref, kb_ref, vb_ref, qb_ref):
    hb = h_ref[...].astype(BF16)
    cos_t, sin_a, sin_b = cos_ref[...], sa_ref[...], sb_ref[...]
    k = _rotary(_dot(hb, wk_ref[...]), cos_t, sin_a, sin_b)
    k_ref[...] = k
    kb_ref[...] = k.astype(BF16)
    v = _dot(hb, wv_ref[...])
    v_ref[...] = v
    vb_ref[...] = v.astype(BF16)
    q = _rotary(_dot(hb, wq_ref[...]), cos_t, sin_a, sin_b)
    qb_ref[...] = q.astype(qb_ref.dtype)


def _kvq(h, wk, wv, wq, cos_t, sin_a, sin_b, tm, q_dtype, dec=None):
    m = h.shape[0]
    tab_blocks = cos_t.shape[0] // tm
    full = pl.BlockSpec((tm, D_MODEL), _ROW)
    tab = pl.BlockSpec((tm, LANES), lambda i, *_: (i % tab_blocks, 0))
    return _row_tiled_call(
        _kvq_kernel, "kvq_proj", (h, wk, wv, wq, cos_t, sin_a, sin_b),
        in_specs=[full, _weight_spec(wk.shape), _weight_spec(wv.shape), _weight_spec(wq.shape), tab, tab, tab],
        out_specs=[full, full, full, full, full],
        out_shape=[jax.ShapeDtypeStruct((m, D_MODEL), F32), jax.ShapeDtypeStruct((m, D_MODEL), F32),
                   jax.ShapeDtypeStruct((m, D_MODEL), BF16), jax.ShapeDtypeStruct((m, D_MODEL), BF16),
                   jax.ShapeDtypeStruct((m, D_MODEL), q_dtype)],
        n_steps=m // tm, dec=dec)


def _rope_tables(pos):
    half = ROT_DIM // 2
    inv_freq = ROPE_THETA ** (-jnp.arange(half, dtype=F32) * 2.0 / ROT_DIM)
    ang = pos.astype(F32)[:, None] * inv_freq[None, :]
    cos, sin = jnp.cos(ang), jnp.sin(ang)
    n = pos.shape[0]
    rest = DIFF_QK_DIM - ROT_DIM
    cos_sub = jnp.concatenate([cos, cos, jnp.ones((n, rest), F32)], axis=1)
    sa_sub = jnp.concatenate([-sin, jnp.zeros((n, half + rest), F32)], axis=1)
    sb_sub = jnp.concatenate([jnp.zeros((n, half), F32), sin, jnp.zeros((n, rest), F32)], axis=1)
    rep = lambda t: jnp.concatenate([t, t], axis=1)
    return rep(cos_sub), rep(sa_sub), rep(sb_sub)


def _diff_lambda(lam_ref, lambda_init):
    lv = lam_ref[...]
    s1 = jnp.sum(lv[0:1] * lv[1:2], axis=-1, keepdims=True)
    s2 = jnp.sum(lv[2:3] * lv[3:4], axis=-1, keepdims=True)
    return jnp.exp(s1) - jnp.exp(s2) + lambda_init


def _fold_lanes(x, op):
    r = x[:, 0:LANES]
    for t in range(1, x.shape[1] // LANES):
        r = op(r, x[:, t * LANES:(t + 1) * LANES])
    return r


def _exp2_minus(s, m_rep):
    return jnp.concatenate([jnp.exp2(s[:, t * LANES:(t + 1) * LANES] - m_rep)
                            for t in range(s.shape[1] // LANES)], axis=1)


def _attn_prompt_kernel(q_ref, k_ref, v_ref, km_ref, vm_ref, lam_ref, sw_ref, o_ref,
                        s_ref, sm_ref, mx_ref, acc_ref, *, lambda_init, meta_pad):
    i = pl.program_id(2)
    tq = ATT_TQ
    lane = lax.broadcasted_iota(jnp.int32, (tq, LANES), 1)
    heads = [hh * LANES for hh in range(ATT_HEADS) for _ in range(2)]
    q_sub = []
    for hh in range(ATT_HEADS):
        q = q_ref[0, :, hh * LANES:(hh + 1) * LANES].astype(F32)
        q_sub.append(jnp.where(lane < DIFF_QK_DIM, q, 0.0).astype(BF16))
        q_sub.append(jnp.where(lane < DIFF_QK_DIM, 0.0, q).astype(BF16))
    n_streams = len(q_sub)

    meta_keep = lax.broadcasted_iota(jnp.int32, (tq, CHUNK), 1) >= meta_pad
    for n in range(n_streams):
        s = jnp.where(meta_keep, _dot_nt(q_sub[n], km_ref[:, heads[n]:heads[n] + LANES]), -jnp.inf)
        sm_ref[n] = s
        mx_ref[n] = s

    def qk_block(j, keep):
        rows = pl.ds(pl.multiple_of(j * tq, tq), tq)
        for n in range(n_streams):
            s = _dot_nt(q_sub[n], k_ref[0, rows, heads[n]:heads[n] + LANES])
            if keep is not None:
                s = jnp.where(keep, s, -jnp.inf)
            s_ref[n, j] = s
            mx_ref[n] = jnp.maximum(mx_ref[n], _fold_lanes(s, jnp.maximum))

    def qk_body(j, carry):
        qk_block(j, None)
        return carry

    lax.fori_loop(0, i, qk_body, 0)
    qk_block(i, lax.broadcasted_iota(jnp.int32, (tq, tq), 1) <= lax.broadcasted_iota(jnp.int32, (tq, tq), 0))

    for n in range(n_streams):
        mx_ref[n] = jnp.broadcast_to(jnp.max(mx_ref[n], axis=-1, keepdims=True), (tq, LANES))

    ones_meta = jnp.ones((CHUNK, LANES), BF16)
    for n in range(n_streams):
        p = jnp.exp2(sm_ref[n] - mx_ref[n])
        v_meta = jnp.concatenate([vm_ref[:, heads[n]:heads[n] + LANES], ones_meta], axis=1)
        acc_ref[n] = _dot(p.astype(BF16), v_meta)

    ones_blk = jnp.ones((tq, LANES), BF16)

    def pv_body(j, carry):
        rows = pl.ds(pl.multiple_of(j * tq, tq), tq)
        for n in range(n_streams):
            vb = jnp.concatenate([v_ref[0, rows, heads[n]:heads[n] + LANES], ones_blk], axis=1)
            p = _exp2_minus(s_ref[n, j], mx_ref[n])
            acc_ref[n] = acc_ref[n] + _dot(p.astype(BF16), vb)
        return carry

    lax.fori_loop(0, i + 1, pv_body, 0)

    lam = _diff_lambda(lam_ref, lambda_init)
    for hh in range(ATT_HEADS):
        acc0, acc1 = acc_ref[2 * hh], acc_ref[2 * hh + 1]
        o = acc0[:, 0:DIFF_V_DIM] / acc0[:, DIFF_V_DIM:] - lam * (acc1[:, 0:DIFF_V_DIM] / acc1[:, DIFF_V_DIM:])
        ms = jnp.mean(o * o, axis=-1, keepdims=True)
        o = o * lax.rsqrt(ms + NORM_EPS) * sw_ref[...] * (1.0 - lambda_init)
        o_ref[0, :, hh * LANES:(hh + 1) * LANES] = o.astype(o_ref.dtype)


def _attn_prompt(q3, k3, v3, k_meta, v_meta, lam_vecs, subln_w, lambda_init, meta_pad):
    nb, seq_len, _ = q3.shape
    n_q = seq_len // ATT_TQ
    kern = functools.partial(_attn_prompt_kernel, lambda_init=lambda_init, meta_pad=meta_pad)
    width = ATT_HEADS * LANES
    n_streams = 2 * ATT_HEADS
    q_spec = pl.BlockSpec((1, ATT_TQ, width), lambda b, h, i: (b, i, h))
    kv_spec = pl.BlockSpec((1, seq_len, width), lambda b, h, i: (b, 0, h))
    meta_spec = pl.BlockSpec((CHUNK, width), lambda b, h, i: (0, h))
    return pl.pallas_call(
        kern,
        grid=(nb, DIFF_HEADS // ATT_HEADS, n_q),
        in_specs=[q_spec, kv_spec, kv_spec, meta_spec, meta_spec,
                  _const_spec(lam_vecs.shape), _const_spec(subln_w.shape)],
        out_specs=q_spec,
        out_shape=jax.ShapeDtypeStruct((nb, seq_len, DIFF_HEADS * DIFF_V_DIM), BF16),
        scratch_shapes=[pltpu.VMEM((n_streams, n_q, ATT_TQ, ATT_TQ), F32),
                        pltpu.VMEM((n_streams, ATT_TQ, CHUNK), F32),
                        pltpu.VMEM((n_streams, ATT_TQ, LANES), F32),
                        pltpu.VMEM((n_streams, ATT_TQ, DIFF_V_DIM + LANES), F32)],
        compiler_params=_cparams(("parallel", "parallel", "arbitrary")),
        name="attn_prompt",
    )(q3, k3, v3, k_meta, v_meta, lam_vecs, subln_w)


N_SUB = 2 * DIFF_HEADS


N_DEC_IN = 2 * DEC_PAGES_PER_STEP + 6
N_DEC_SCRATCH = 3


def _decode_attn_init(j, scratch):
    m_ref, l_ref, acc_ref = scratch

    @pl.when(j == 0)
    def _():
        m_ref[...] = jnp.full(m_ref.shape, -jnp.inf, F32)
        l_ref[...] = jnp.zeros(l_ref.shape, F32)
        acc_ref[...] = jnp.zeros(acc_ref.shape, F32)


def _decode_attn_step(j, n_steps, refs, o_ref, scratch, lambda_init):
    npg = DEC_PAGES_PER_STEP
    k_refs = refs[0:npg]
    v_refs = refs[npg:2 * npg]
    q_ref, kn_ref, vn_ref, rep_ref, lam_ref, sw_ref = refs[2 * npg:]
    m_ref, l_ref, acc_ref = scratch

    row = lax.broadcasted_iota(jnp.int32, (N_SUB, HQK), 0)
    col = lax.broadcasted_iota(jnp.int32, (N_SUB, HQK), 1)
    row_head = row % DIFF_HEADS
    q_mask = (col // DIFF_QK_DIM) == (2 * row_head + row // DIFF_HEADS)
    p_mask = (col % DIFF_HEADS) == row_head
    q_bd = jnp.where(q_mask, q_ref[0], 0.0)
    q_bdb = q_bd.astype(BF16)

    s = [_dot(q_bdb, k_refs[u][0].astype(BF16)) for u in range(npg)]
    m_old = m_ref[...]
    m_new = m_old
    for u in range(npg):
        m_new = jnp.maximum(m_new, jnp.max(s[u], axis=-1, keepdims=True))
    alpha = jnp.exp2(m_old - m_new)
    p = [jnp.exp2(s[u] - m_new) for u in range(npg)]
    l_new = alpha * l_ref[...]
    for u in range(npg):
        l_new = l_new + jnp.sum(p[u], axis=-1, keepdims=True)
    p_rep = _dot(jnp.concatenate(p, axis=0).astype(BF16), rep_ref[...])
    acc = alpha * acc_ref[...]
    for u in range(npg):
        p_exp = jnp.where(p_mask, p_rep[u * N_SUB:(u + 1) * N_SUB], 0.0).astype(BF16)
        acc = acc + _dot(p_exp, v_refs[u][0].astype(BF16))
    m_ref[...] = m_new
    l_ref[...] = l_new
    acc_ref[...] = acc

    @pl.when(j == n_steps - 1)
    def _():
        s_new = jnp.sum(q_bd * kn_ref[0], axis=-1, keepdims=True)
        m_fin = jnp.maximum(m_new, s_new)
        a_fin = jnp.exp2(m_new - m_fin)
        p_new = jnp.exp2(s_new - m_fin)
        l_fin = a_fin * l_new + p_new
        v_new = jnp.concatenate([vn_ref[0], vn_ref[0]], axis=0)
        out = (a_fin * acc + p_new * v_new) / l_fin
        lam = _diff_lambda(lam_ref, lambda_init)
        o = out[0:DIFF_HEADS] - lam * out[DIFF_HEADS:N_SUB]
        ms = jnp.mean(o * o, axis=-1, keepdims=True)
        o_ref[0] = (o * lax.rsqrt(ms + NORM_EPS) * sw_ref[...] * (1.0 - lambda_init)).astype(o_ref.dtype)


def _row_tiled_call(host_kernel, name, args, in_specs, out_specs, out_shape, n_steps, dec=None):
    if dec is None:
        return pl.pallas_call(host_kernel, grid=(n_steps,), in_specs=in_specs, out_specs=out_specs,
                              out_shape=out_shape, compiler_params=_cparams(("parallel",)), name=name)(*args)
    seq0, page_table, k_pages, v_pages, q, k_new, v_new, lam_vecs, subln_w, lambda_init = dec
    npg = DEC_PAGES_PER_STEP
    spq = page_table.shape[1] // npg
    n_seq = n_steps // spq
    assert n_seq * spq == n_steps and v_pages.shape[1:] == k_pages.shape[1:]
    page_blk = (1,) + k_pages.shape[1:]

    def page_spec(u):
        return pl.BlockSpec(page_blk, lambda t, pt: (pt[seq0 + t // spq, (t % spq) * npg + u], 0, 0))

    per_seq = lambda t, pt: (seq0 + t // spq, 0, 0)
    const2 = lambda t, pt: (0, 0)
    rep = (jnp.arange(PAGE_SIZE)[:, None] == jnp.arange(PAGE_SIZE * DIFF_HEADS)[None, :] // DIFF_HEADS).astype(BF16)
    dec_specs = [page_spec(u) for u in range(npg)] + [page_spec(u) for u in range(npg)] + [
        pl.BlockSpec((1, 1, HQK), per_seq), pl.BlockSpec((1, 1, HQK), per_seq),
        pl.BlockSpec((1, DIFF_HEADS, DIFF_V_DIM), per_seq),
        pl.BlockSpec(rep.shape, const2), pl.BlockSpec(lam_vecs.shape, const2), pl.BlockSpec(subln_w.shape, const2)]
    dec_args = [k_pages] * npg + [v_pages] * npg + [q, k_new, v_new, rep, lam_vecs, subln_w]
    n_in, n_out = len(in_specs), len(out_specs)

    def kern(pt_ref, *refs):
        j = pl.program_id(0) % spq
        scratch = refs[n_in + N_DEC_IN + n_out + 1:]
        _decode_attn_init(j, scratch)
        host_kernel(*refs[:n_in], *refs[n_in + N_DEC_IN:n_in + N_DEC_IN + n_out])
        _decode_attn_step(j, spq, refs[n_in:n_in + N_DEC_IN], refs[n_in + N_DEC_IN + n_out], scratch, lambda_init)

    grid_spec = pltpu.PrefetchScalarGridSpec(
        num_scalar_prefetch=1,
        grid=(n_steps,),
        in_specs=list(in_specs) + dec_specs,
        out_specs=list(out_specs) + [pl.BlockSpec((1, DIFF_HEADS, DIFF_V_DIM), lambda t, pt: (t // spq, 0, 0))],
        scratch_shapes=[pltpu.VMEM((N_SUB, 1), F32), pltpu.VMEM((N_SUB, 1), F32),
                        pltpu.VMEM((N_SUB, DIFF_V_DIM), F32)],
    )
    return pl.pallas_call(
        kern,
        grid_spec=grid_spec,
        out_shape=list(out_shape) + [jax.ShapeDtypeStruct((n_seq, DIFF_HEADS, DIFF_V_DIM), BF16)],
        compiler_params=_cparams(("arbitrary",)),
        name=name + "_dec",
    )(page_table, *args, *dec_args)


def kernel(x_prompt, x_sample, cache_k, cache_v, state_ssm, state_conv, page_table, meta_tokens, a_w_in, a_conv_w, a_conv_b, a_dt_bias, a_A_log, a_D, a_norm_w, a_w_out, kv_w_k, kv_w_v, b_w_q, b_lambda, b_subln_w, b_w_o, ffn_w_gate, ffn_w_up, ffn_w_down, ln_mix_w, ln_mix_b, ln_ffn_w, ln_ffn_b):
    bp, seq, _ = x_prompt.shape
    bs = x_sample.shape[0]
    n_pages = page_table.shape[1]
    past_len = n_pages * PAGE_SIZE
    tm_p = ROW_TILE
    lambda_init = 0.8 - 0.6 * math.exp(-0.3 * 1)
    scale = DIFF_QK_DIM ** -0.5

    w_in = a_w_in[0]
    wz = w_in[:, :D_INNER].astype(BF16)
    wx = w_in[:, D_INNER:D_INNER + CONV_DIM].astype(BF16)
    wdt = jnp.pad(w_in[:, D_INNER + CONV_DIM:], ((0, 0), (0, LANES - SSM_HEADS))).astype(BF16)
    pad_h = lambda v: jnp.pad(v.reshape(1, SSM_HEADS), ((0, 0), (0, LANES - SSM_HEADS)))
    dt_bias, a_log = pad_h(a_dt_bias[0]), pad_h(a_A_log[0])
    conv_w, conv_b = a_conv_w[0], a_conv_b[0].reshape(1, CONV_DIM)
    d_exp = jnp.repeat(a_D[0], SSM_HEAD_DIM).reshape(1, D_INNER)
    norm_w = a_norm_w[0].reshape(1, D_INNER)
    w_out = a_w_out[0].astype(BF16)
    wk, wv = kv_w_k.astype(BF16), kv_w_v.astype(BF16)
    wq = (b_w_q[0] * (scale * LOG2_E)).astype(BF16)
    wo = b_w_o[0].astype(BF16)
    wg, wu, wd = ffn_w_gate.astype(BF16), ffn_w_up.astype(BF16), ffn_w_down.astype(BF16)
    ln = lambda t, l: t[l].reshape(1, D_MODEL)
    subln_w = b_subln_w[0].reshape(1, DIFF_V_DIM)
    lam_vecs = b_lambda[0]

    n_main = bp * seq
    n_aux = CHUNK + bs
    hm = x_prompt.reshape(n_main, D_MODEL)
    ha = jnp.concatenate([jnp.zeros((PAD_ROWS, D_MODEL), F32), meta_tokens, x_sample.reshape(bs, D_MODEL)], axis=0)
    smp = lambda t: t[CHUNK:]

    z_a, xbc_a, dtr_a = _in_proj(ha, wz, wx, wdt, n_aux)
    y_meta, tail_meta, _, st_meta = _mamba_prompt(
        xbc_a, z_a, dtr_a, jnp.zeros((SUBLANES, CONV_DIM), F32), jnp.zeros((SSM_STATE, D_INNER), F32),
        conv_w, conv_b, dt_bias, a_log, d_exp, norm_w, 1, 1, PAD_ROWS)
    prev = jnp.transpose(state_conv[0], (1, 0, 2))
    xc_s, conv_s, dt_s, decay_s = _conv_step(smp(xbc_a), prev, conv_w, conv_b, smp(dtr_a), dt_bias, a_log)
    per_ch = lambda t: jnp.repeat(t[:, :SSM_HEADS], SSM_HEAD_DIM, axis=1)
    y_s, ssm_s = _ssd_step(xc_s[:, :D_INNER], smp(z_a), per_ch(dt_s), per_ch(decay_s),
                           xc_s[:, D_INNER:D_INNER + GN].reshape(bs, SSM_GROUPS, SSM_STATE),
                           xc_s[:, D_INNER + GN:].reshape(bs, SSM_GROUPS, SSM_STATE),
                           state_ssm[0], d_exp, norm_w)
    y_a = jnp.concatenate([y_meta, y_s], axis=0)

    ha = _proj_ffn_ln(y_a, ha, w_out, ln(ln_mix_w, 0), ln(ln_mix_b, 0),
                      wg[0], wu[0], wd[0], ln(ln_ffn_w, 0), ln(ln_ffn_b, 0), n_aux)
    pos_a = jnp.concatenate([jnp.zeros((PAD_ROWS,), jnp.int32), jnp.arange(N_META),
                             jnp.full((bs,), past_len, jnp.int32)])
    k_a, v_a, kb_a, vb_a, q_a = _kvq(ha, wk, wv, wq, *_rope_tables(pos_a), n_aux, F32)
    hs, k_s, v_s, q_s = smp(ha), smp(k_a), smp(v_a), smp(q_a)

    z_m, xbc_m, dtr_m = _in_proj(hm, wz, wx, wdt, tm_p)
    y_m, conv_tail, ssm_p, _ = _mamba_prompt(xbc_m, z_m, dtr_m, tail_meta[0], st_meta[0],
                                             conv_w, conv_b, dt_bias, a_log, d_exp, norm_w, bp, seq // CHUNK, 0)
    hm = _proj_ffn_ln(y_m, hm, w_out, ln(ln_mix_w, 0), ln(ln_mix_b, 0),
                      wg[0], wu[0], wd[0], ln(ln_ffn_w, 0), ln(ln_ffn_b, 0), tm_p)
    k_m, v_m, kb_m, vb_m, qb_m = _kvq(hm, wk, wv, wq, *_rope_tables(N_META + jnp.arange(seq)), tm_p, BF16)

    to3 = lambda t: t.reshape(bp, seq, D_MODEL)
    o_m = _attn_prompt(to3(qb_m), to3(kb_m), to3(vb_m), kb_a, vb_a, lam_vecs, subln_w, lambda_init, PAD_ROWS)
    hm = _proj_ffn_ln(o_m.reshape(n_main, D_MODEL), hm, wo, ln(ln_mix_w, 1), ln(ln_mix_b, 1),
                      wg[1], wu[1], wd[1], ln(ln_ffn_w, 1), ln(ln_ffn_b, 1), tm_p)

    n_pool = cache_k.shape[0]
    k_pages = jnp.transpose(cache_k, (0, 2, 3, 4, 1)).reshape(n_pool, HQK, PAGE_SIZE)
    v_pages = cache_v.reshape(n_pool, PAGE_SIZE * DIFF_HEADS, DIFF_V_DIM)
    dec = (0, page_table, k_pages, v_pages, q_s.reshape(bs, 1, HQK), k_s.reshape(bs, 1, HQK),
           v_s.reshape(bs, DIFF_HEADS, DIFF_V_DIM), lam_vecs, subln_w, lambda_init)
    o_s, = _row_tiled_call(lambda: None, "attn", (), [], [], [], bs * (n_pages // DEC_PAGES_PER_STEP), dec)
    hs = _proj_ffn_ln(o_s.reshape(bs, D_MODEL), hs, wo, ln(ln_mix_w, 1), ln(ln_mix_b, 1),
                      wg[1], wu[1], wd[1], ln(ln_ffn_w, 1), ln(ln_ffn_b, 1), bs)

    with_meta = lambda meta, main: jnp.concatenate(
        [jnp.broadcast_to(meta[None, PAD_ROWS:CHUNK], (bp, N_META, D_MODEL)), to3(main)], axis=1)
    y_prompt = to3(hm)
    k_prompt = with_meta(k_a, k_m).reshape(bp, N_META + seq, DIFF_HEADS, 2, DIFF_QK_DIM)
    v_prompt = with_meta(v_a, v_m).reshape(bp, N_META + seq, DIFF_HEADS, DIFF_V_DIM)
    ssm_prompt = ssm_p[None]
    conv_prompt = conv_tail[None, :, SUBLANES - (D_CONV - 1):]
    y_sample = hs.reshape(bs, 1, D_MODEL)
    k_sample = k_s.reshape(bs, 1, DIFF_HEADS, 2, DIFF_QK_DIM)
    v_sample = v_s.reshape(bs, 1, DIFF_HEADS, DIFF_V_DIM)
    ssm_sample = ssm_s[None]
    conv_sample = jnp.transpose(conv_s, (1, 0, 2))[None]
    return (y_prompt, y_sample, k_prompt, v_prompt, ssm_prompt, conv_prompt, k_sample, v_sample, ssm_sample,
            conv_sample)
```

```python
import functools
import math

import jax
import jax.numpy as jnp
from jax import lax
from jax.experimental import pallas as pl
from jax.experimental.pallas import tpu as pltpu

F32 = jnp.float32
BF16 = jnp.bfloat16

D_MODEL = 1024
N_META = 16
D_INNER = 2048
SSM_HEADS = 32
SSM_HEAD_DIM = 64
SSM_GROUPS = 4
SSM_STATE = 128
D_CONV = 4
GN = SSM_GROUPS * SSM_STATE
CONV_DIM = D_INNER + 2 * GN
DIFF_HEADS = 8
DIFF_QK_DIM = 64
DIFF_V_DIM = 128
HQK = DIFF_HEADS * 2 * DIFF_QK_DIM
ROT_DIM = DIFF_QK_DIM // 4
ROPE_THETA = 500000.0
D_FF = 2816
DEPTH = 2
DEEPNORM_ALPHA = (2 * DEPTH) ** 0.25
NORM_EPS = 1e-5
PAGE_SIZE = 128
LOG2_E = math.log2(math.e)

LANES = 128
SUBLANES = 8
VMEM_LIMIT_BYTES = 56 * 1024 * 1024

CHUNK = 128
CONV_COLS = 512
PAD_ROWS = CHUNK - N_META
ATT_TQ = 512
ATT_HEADS = 2
DEC_PAGES_PER_STEP = 8
ROW_TILE = 512


def _cparams(sem):
    return pltpu.CompilerParams(dimension_semantics=sem, vmem_limit_bytes=VMEM_LIMIT_BYTES)


def _const_spec(shape):
    nd = len(shape)
    return pl.BlockSpec(shape, lambda *_: (0,) * nd)


def _weight_spec(shape):
    nd = len(shape)
    return pl.BlockSpec(shape, lambda *_: (0,) * nd, pipeline_mode=pl.Buffered(1))


def _ROW(i, *_):
    return (i, 0)


def _layer_norm(x, w, b):
    mu = jnp.mean(x, axis=-1, keepdims=True)
    xc = x - mu
    var = jnp.mean(xc * xc, axis=-1, keepdims=True)
    return xc * lax.rsqrt(var + NORM_EPS) * w + b


def _silu(x):
    h = 0.5 * x
    return h + h * jnp.tanh(h)


def _softplus(x):
    return jnp.maximum(x, 0.0) + jnp.log1p(jnp.exp(-jnp.abs(x)))


def _dot(a, b):
    return jnp.dot(a, b, preferred_element_type=F32)


def _dot_nt(a, b):
    return lax.dot_general(a, b, (((1,), (1,)), ((), ())), preferred_element_type=F32)


NCHUNK = 512


def _in_proj_kernel(x_ref, wz_ref, wx_ref, wdt_ref, z_ref, xbc_ref, dt_ref):
    xb = x_ref[...].astype(BF16)
    for n0 in range(0, D_INNER, NCHUNK):
        z_ref[:, n0:n0 + NCHUNK] = _dot(xb, wz_ref[:, n0:n0 + NCHUNK])
    for n0 in range(0, CONV_DIM, NCHUNK):
        xbc_ref[:, n0:n0 + NCHUNK] = _dot(xb, wx_ref[:, n0:n0 + NCHUNK])
    dt_ref[...] = _dot(xb, wdt_ref[...])


def _in_proj(x, wz, wx, wdt, tm):
    m = x.shape[0]
    return _row_tiled_call(
        _in_proj_kernel, "in_proj", (x, wz, wx, wdt),
        in_specs=[pl.BlockSpec((tm, D_MODEL), _ROW), _weight_spec(wz.shape), _weight_spec(wx.shape),
                  _weight_spec(wdt.shape)],
        out_specs=[pl.BlockSpec((tm, D_INNER), _ROW), pl.BlockSpec((tm, CONV_DIM), _ROW),
                   pl.BlockSpec((tm, LANES), _ROW)],
        out_shape=[jax.ShapeDtypeStruct((m, D_INNER), F32), jax.ShapeDtypeStruct((m, CONV_DIM), F32),
                   jax.ShapeDtypeStruct((m, LANES), F32)],
        n_steps=m // tm)


def _mamba_prompt_kernel(xbc_ref, z_ref, dtr_ref, tail0_ref, st0_ref, cw_ref, cb_ref, dtb_ref, alog_ref, dexp_ref,
                         nw_ref, y_ref, conv_out_ref, ssm_out_ref, st_out_ref,
                         ext_ref, st_ref, xc_ref, ybuf_ref, *, pad_rows):
    c = pl.program_id(1)
    n_chunks = pl.num_programs(1)

    @pl.when(c == 0)
    def _():
        ext_ref[0:SUBLANES, :] = tail0_ref[...]
        st_ref[...] = st0_ref[...]

    x_raw = xbc_ref[...]
    ext_ref[SUBLANES:SUBLANES + CHUNK, :] = x_raw
    for n0 in range(0, CONV_DIM, CONV_COLS):
        cols = slice(n0, n0 + CONV_COLS)
        ext = ext_ref[:, cols]
        prev1 = pltpu.roll(ext, 1, 0)
        u = cw_ref[1:2, cols] * ext + cw_ref[0:1, cols] * prev1
        conv = (cb_ref[:, cols] + cw_ref[3:4, cols] * ext[SUBLANES:] + cw_ref[2:3, cols] * prev1[SUBLANES:]
                + pltpu.roll(u, 2, 0)[SUBLANES:])
        xc_ref[:, cols] = _silu(conv)
    ext_ref[0:SUBLANES, :] = x_raw[CHUNK - SUBLANES:CHUNK, :]

    @pl.when(c == n_chunks - 1)
    def _():
        conv_out_ref[0] = x_raw[CHUNK - SUBLANES:CHUNK, :]

    row = lax.broadcasted_iota(jnp.int32, (CHUNK, LANES), 0)
    col = lax.broadcasted_iota(jnp.int32, (CHUNK, LANES), 1)
    dt = _softplus(dtr_ref[...] + dtb_ref[...])
    if pad_rows:
        dt = jnp.where(jnp.logical_or(c > 0, row >= pad_rows), dt, 0.0)
    a = dt * (-jnp.exp(alog_ref[...]))
    tri = (row >= col).astype(F32)
    cs = jnp.dot(tri, a, preferred_element_type=F32, precision=lax.Precision.HIGHEST)
    cs_t = cs.T
    dt_t = dt.T
    e_cs = jnp.exp(cs)
    causal = row >= col
    lane_lo = col < SSM_HEAD_DIM

    for g in range(SSM_GROUPS):
        b_g = xc_ref[:, D_INNER + g * SSM_STATE:D_INNER + (g + 1) * SSM_STATE]
        c_g = xc_ref[:, D_INNER + GN + g * SSM_STATE:D_INNER + GN + (g + 1) * SSM_STATE]
        c_gb = c_g.astype(BF16)
        b_gb = b_g.astype(BF16)
        cb = _dot_nt(c_gb, b_gb)
        bt = b_g.T
        g0 = g * (D_INNER // SSM_GROUPS)
        y_off_g = _dot(c_gb, st_ref[:, g0:g0 + D_INNER // SSM_GROUPS].astype(BF16))
        for jp in range(D_INNER // SSM_GROUPS // LANES):
            lo = g0 + jp * LANES
            h0 = lo // SSM_HEAD_DIM
            m_parts, mp_parts, dec_parts, ecol_parts = [], [], [], []
            for h in (h0, h0 + 1):
                colv = cs[:, h:h + 1]
                rowv = cs_t[h:h + 1, :]
                dtrow = dt_t[h:h + 1, :]
                last = cs[CHUNK - 1:CHUNK, h:h + 1]
                lmat = jnp.exp(jnp.where(causal, colv - rowv, -jnp.inf))
                m_parts.append((cb * lmat * dtrow).astype(BF16))
                mp_parts.append((bt * (jnp.exp(last - rowv) * dtrow)).astype(BF16))
                dec_parts.append(jnp.exp(last))
                ecol_parts.append(e_cs[:, h:h + 1])
            lhs = jnp.concatenate([jnp.concatenate(m_parts, axis=1),
                                   jnp.concatenate(mp_parts, axis=1)], axis=0)
            x_pair = xc_ref[:, lo:lo + LANES]
            rhs = jnp.concatenate([jnp.where(lane_lo, x_pair, 0.0).astype(BF16),
                                   jnp.where(lane_lo, 0.0, x_pair).astype(BF16)], axis=0)
            res = _dot(lhs, rhs)
            y_diag = res[0:CHUNK]
            d_state = res[CHUNK:2 * CHUNK]
            decay = jnp.where(lane_lo, dec_parts[0], dec_parts[1])
            ecol = jnp.where(lane_lo, ecol_parts[0], ecol_parts[1])
            st_ref[:, lo:lo + LANES] = st_ref[:, lo:lo + LANES] * decay + d_state
            y_pair = y_diag + y_off_g[:, jp * LANES:(jp + 1) * LANES] * ecol + x_pair * dexp_ref[:, lo:lo + LANES]
            ybuf_ref[:, lo:lo + LANES] = y_pair

    gated = ybuf_ref[...] * _silu(z_ref[...])
    ms = jnp.mean(gated * gated, axis=-1, keepdims=True)
    y_ref[...] = (gated * lax.rsqrt(ms + NORM_EPS) * nw_ref[...]).astype(y_ref.dtype)

    @pl.when(c == n_chunks - 1)
    def _():
        st_out_ref[0] = st_ref[...]
        for jp in range(D_INNER // LANES):
            t = st_ref[:, jp * LANES:(jp + 1) * LANES].T
            ssm_out_ref[0, 2 * jp] = t[0:SSM_HEAD_DIM]
            ssm_out_ref[0, 2 * jp + 1] = t[SSM_HEAD_DIM:2 * SSM_HEAD_DIM]


def _mamba_prompt(xbc, z, dt_raw, tail0, st0, conv_w, conv_b, dt_bias, a_log, d_exp, norm_w, nb, n_chunks, pad_rows,
                  dec=None):
    m = nb * n_chunks * CHUNK
    blk = lambda b, c, *_: (b * n_chunks + c, 0)
    per_seq3 = lambda b, c, *_: (b, 0, 0)
    return _fused_call(
        functools.partial(_mamba_prompt_kernel, pad_rows=pad_rows), "mamba_prompt", (nb, n_chunks),
        ("parallel", "arbitrary"), (xbc, z, dt_raw, tail0, st0, conv_w, conv_b, dt_bias, a_log, d_exp, norm_w),
        in_specs=[pl.BlockSpec((CHUNK, CONV_DIM), blk), pl.BlockSpec((CHUNK, D_INNER), blk),
                  pl.BlockSpec((CHUNK, LANES), blk), _const_spec(tail0.shape), _const_spec(st0.shape),
                  _const_spec(conv_w.shape), _const_spec(conv_b.shape), _const_spec(dt_bias.shape),
                  _const_spec(a_log.shape), _const_spec(d_exp.shape), _const_spec(norm_w.shape)],
        out_specs=[pl.BlockSpec((CHUNK, D_INNER), blk),
                   pl.BlockSpec((1, SUBLANES, CONV_DIM), per_seq3),
                   pl.BlockSpec((1, SSM_HEADS, SSM_HEAD_DIM, SSM_STATE), lambda b, c, *_: (b, 0, 0, 0)),
                   pl.BlockSpec((1, SSM_STATE, D_INNER), per_seq3)],
        out_shape=[jax.ShapeDtypeStruct((m, D_INNER), BF16),
                   jax.ShapeDtypeStruct((nb, SUBLANES, CONV_DIM), F32),
                   jax.ShapeDtypeStruct((nb, SSM_HEADS, SSM_HEAD_DIM, SSM_STATE), F32),
                   jax.ShapeDtypeStruct((nb, SSM_STATE, D_INNER), F32)],
        scratch_shapes=[pltpu.VMEM((SUBLANES + CHUNK, CONV_DIM), F32),
                        pltpu.VMEM((SSM_STATE, D_INNER), F32),
                        pltpu.VMEM((CHUNK, CONV_DIM), F32),
                        pltpu.VMEM((CHUNK, D_INNER), F32)],
        dec=dec)


def _conv_step_kernel(xbc_ref, prev_ref, cw_ref, cb_ref, dtr_ref, dtb_ref, alog_ref,
                      xc_ref, newconv_ref, dt_ref, decay_ref):
    x_raw = xbc_ref[...]
    conv = cb_ref[...] + cw_ref[3:4, :] * x_raw
    for k in range(D_CONV - 1):
        conv = conv + cw_ref[k:k + 1, :] * prev_ref[k]
    xc_ref[...] = _silu(conv)
    newconv_ref[0] = prev_ref[1]
    newconv_ref[1] = prev_ref[2]
    newconv_ref[2] = x_raw
    dt = _softplus(dtr_ref[...] + dtb_ref[...])
    dt_ref[...] = dt
    decay_ref[...] = jnp.exp(dt * (-jnp.exp(alog_ref[...])))


def _conv_step(xbc, prev, conv_w, conv_b, dt_raw, dt_bias, a_log):
    nb = xbc.shape[0]
    return pl.pallas_call(
        _conv_step_kernel,
        out_shape=[jax.ShapeDtypeStruct((nb, CONV_DIM), F32), jax.ShapeDtypeStruct((D_CONV - 1, nb, CONV_DIM), F32),
                   jax.ShapeDtypeStruct((nb, LANES), F32), jax.ShapeDtypeStruct((nb, LANES), F32)],
        compiler_params=pltpu.CompilerParams(vmem_limit_bytes=VMEM_LIMIT_BYTES),
        name="conv_step",
    )(xbc, prev, conv_w, conv_b, dt_raw, dt_bias, a_log)


N_PAIRS = D_INNER // LANES


def _ssd_step_kernel(x_ref, z_ref, dt_ref, dec_ref, b_ref, c_ref, h_ref, dsk_ref, nw_ref, y_ref, hout_ref):
    x = x_ref[0]
    pad = jnp.zeros((LANES - N_PAIRS, LANES), F32)
    cols = lambda v: jnp.concatenate([v, pad], axis=0).T
    x_t, dtx_t, dec_t = cols(x), cols(dt_ref[0] * x), cols(dec_ref[0])
    lane = lax.broadcasted_iota(jnp.int32, (LANES, LANES), 1)
    y_t = jnp.zeros((LANES, LANES), F32)
    for r in range(N_PAIRS):
        g = r // (N_PAIRS // SSM_GROUPS)
        b_row = b_ref[0, g:g + 1, :]
        c_row = c_ref[0, g:g + 1, :]
        dtx_col, dec_col = dtx_t[:, r:r + 1], dec_t[:, r:r + 1]
        h0 = h_ref[0, r]
        cb = jnp.sum(c_row * b_row, axis=-1, keepdims=True)
        y_col = dec_col * jnp.sum(h0 * c_row, axis=-1, keepdims=True) + cb * dtx_col
        hout_ref[0, r] = dec_col * h0 + dtx_col * b_row
        y_t = jnp.where(lane == r, y_col, y_t)
    y = y_t.T[0:N_PAIRS] + x * dsk_ref[...]
    gated = y * _silu(z_ref[0])
    ms = jnp.sum(jnp.sum(gated * gated, axis=-1, keepdims=True), axis=0, keepdims=True) / D_INNER
    y_ref[0] = (gated * lax.rsqrt(ms + NORM_EPS) * nw_ref[...]).astype(y_ref.dtype)


def _ssd_step(x, z, dt_ch, decay_ch, b3, c3, h0, d_skip, norm_w):
    nb = x.shape[0]
    per_b = lambda b: (b, 0, 0)
    vec = pl.BlockSpec((1, N_PAIRS, LANES), per_b)
    pairs = lambda t: t.reshape(-1, N_PAIRS, LANES)
    st_spec = pl.BlockSpec((1, N_PAIRS, LANES, SSM_STATE), lambda b: (b, 0, 0, 0))
    grp = pl.BlockSpec((1, SSM_GROUPS, SSM_STATE), per_b)
    y, h_new = pl.pallas_call(
        _ssd_step_kernel,
        grid=(nb,),
        in_specs=[vec, vec, vec, vec, grp, grp, st_spec, _const_spec((N_PAIRS, LANES)), _const_spec((N_PAIRS, LANES))],
        out_specs=[vec, st_spec],
        out_shape=[jax.ShapeDtypeStruct((nb, N_PAIRS, LANES), BF16),
                   jax.ShapeDtypeStruct((nb, N_PAIRS, LANES, SSM_STATE), F32)],
        compiler_params=_cparams(("parallel",)),
        name="ssd_step",
    )(pairs(x), pairs(z), pairs(dt_ch), pairs(decay_ch), b3, c3, h0.reshape(nb, N_PAIRS, LANES, SSM_STATE),
      d_skip.reshape(N_PAIRS, LANES), norm_w.reshape(N_PAIRS, LANES))
    return y.reshape(nb, D_INNER), h_new.reshape(h0.shape)


FF_CHUNK = 256


def _proj_ffn_ln_kernel(a_ref, r_ref, w_ref, mw_ref, mb_ref, wg_ref, wu_ref, wd_ref, fw_ref, fb_ref, o_ref):
    mix = _dot(a_ref[...], w_ref[...])
    x = _layer_norm(DEEPNORM_ALPHA * r_ref[...] + mix, mw_ref[...], mb_ref[...])
    xb = x.astype(BF16)
    acc = jnp.zeros(x.shape, F32)
    for f0 in range(0, D_FF, FF_CHUNK):
        gate = _dot(xb, wg_ref[:, f0:f0 + FF_CHUNK])
        up = _dot(xb, wu_ref[:, f0:f0 + FF_CHUNK])
        act = (_silu(gate) * up).astype(BF16)
        acc = acc + _dot(act, wd_ref[f0:f0 + FF_CHUNK, :])
    o_ref[...] = _layer_norm(DEEPNORM_ALPHA * x + acc, fw_ref[...], fb_ref[...])


def _proj_ffn_ln(a, resid, w, mix_w, mix_b, wg, wu, wd, ffn_w, ffn_b, tm):
    m, k = a.shape
    vec = _const_spec((1, D_MODEL))
    return _row_tiled_call(
        _proj_ffn_ln_kernel, "proj_ffn_ln", (a, resid, w, mix_w, mix_b, wg, wu, wd, ffn_w, ffn_b),
        in_specs=[pl.BlockSpec((tm, k), _ROW), pl.BlockSpec((tm, D_MODEL), _ROW), _weight_spec(w.shape), vec, vec,
                  _weight_spec(wg.shape), _weight_spec(wu.shape), _weight_spec(wd.shape), vec, vec],
        out_specs=[pl.BlockSpec((tm, D_MODEL), _ROW)],
        out_shape=[jax.ShapeDtypeStruct((m, D_MODEL), F32)],
        n_steps=m // tm)[0]


def _rotary(x, cos_t, sin_a, sin_b):
    half = ROT_DIM // 2
    parts = []
    for j in range(x.shape[1] // LANES):
        xb = x[:, j * LANES:(j + 1) * LANES]
        fwd = pltpu.roll(xb, LANES - half, 1)
        bwd = pltpu.roll(xb, half, 1)
        parts.append(xb * cos_t + fwd * sin_a + bwd * sin_b)
    return jnp.concatenate(parts, axis=1)


def _kvq_kernel(h_ref, wk_ref, wv_ref, wq_ref, cos_ref, sa_ref, sb_ref, k_ref, v_ref, kb_ref, vb_ref, qb_ref):
    hb = h_ref[...].astype(BF16)
    cos_t, sin_a, sin_b = cos_ref[...], sa_ref[...], sb_ref[...]
    k = _rotary(_dot(hb, wk_ref[...]), cos_t, sin_a, sin_b)
    k_ref[...] = k
    kb_ref[...] = k.astype(BF16)
    v = _dot(hb, wv_ref[...])
    v_ref[...] = v
    vb_ref[...] = v.astype(BF16)
    q = _rotary(_dot(hb, wq_ref[...]), cos_t, sin_a, sin_b)
    qb_ref[...] = q.astype(qb_ref.dtype)


def _kvq(h, wk, wv, wq, cos_t, sin_a, sin_b, tm, q_dtype):
    m = h.shape[0]
    tab_blocks = cos_t.shape[0] // tm
    full = pl.BlockSpec((tm, D_MODEL), _ROW)
    tab = pl.BlockSpec((tm, LANES), lambda i, *_: (i % tab_blocks, 0))
    return _row_tiled_call(
        _kvq_kernel, "kvq_proj", (h, wk, wv, wq, cos_t, sin_a, sin_b),
        in_specs=[full, _weight_spec(wk.shape), _weight_spec(wv.shape), _weight_spec(wq.shape), tab, tab, tab],
        out_specs=[full, full, full, full, full],
        out_shape=[jax.ShapeDtypeStruct((m, D_MODEL), F32), jax.ShapeDtypeStruct((m, D_MODEL), F32),
                   jax.ShapeDtypeStruct((m, D_MODEL), BF16), jax.ShapeDtypeStruct((m, D_MODEL), BF16),
                   jax.ShapeDtypeStruct((m, D_MODEL), q_dtype)],
        n_steps=m // tm)


def _rope_tables(pos):
    half = ROT_DIM // 2
    inv_freq = ROPE_THETA ** (-jnp.arange(half, dtype=F32) * 2.0 / ROT_DIM)
    ang = pos.astype(F32)[:, None] * inv_freq[None, :]
    cos, sin = jnp.cos(ang), jnp.sin(ang)
    n = pos.shape[0]
    rest = DIFF_QK_DIM - ROT_DIM
    cos_sub = jnp.concatenate([cos, cos, jnp.ones((n, rest), F32)], axis=1)
    sa_sub = jnp.concatenate([-sin, jnp.zeros((n, half + rest), F32)], axis=1)
    sb_sub = jnp.concatenate([jnp.zeros((n, half), F32), sin, jnp.zeros((n, rest), F32)], axis=1)
    rep = lambda t: jnp.concatenate([t, t], axis=1)
    return rep(cos_sub), rep(sa_sub), rep(sb_sub)


def _diff_lambda(lam_ref, lambda_init):
    lv = lam_ref[...]
    s1 = jnp.sum(lv[0:1] * lv[1:2], axis=-1, keepdims=True)
    s2 = jnp.sum(lv[2:3] * lv[3:4], axis=-1, keepdims=True)
    return jnp.exp(s1) - jnp.exp(s2) + lambda_init


def _fold_lanes(x, op):
    r = x[:, 0:LANES]
    for t in range(1, x.shape[1] // LANES):
        r = op(r, x[:, t * LANES:(t + 1) * LANES])
    return r


def _exp2_minus(s, m_rep):
    return jnp.concatenate([jnp.exp2(s[:, t * LANES:(t + 1) * LANES] - m_rep)
                            for t in range(s.shape[1] // LANES)], axis=1)


def _attn_prompt_kernel(q_ref, k_ref, v_ref, km_ref, vm_ref, lam_ref, sw_ref, o_ref,
                        s_ref, sm_ref, mx_ref, acc_ref, *, lambda_init, meta_pad):
    i = pl.program_id(2)
    tq = ATT_TQ
    lane = lax.broadcasted_iota(jnp.int32, (tq, LANES), 1)
    heads = [hh * LANES for hh in range(ATT_HEADS) for _ in range(2)]
    q_sub = []
    for hh in range(ATT_HEADS):
        q = q_ref[0, :, hh * LANES:(hh + 1) * LANES].astype(F32)
        q_sub.append(jnp.where(lane < DIFF_QK_DIM, q, 0.0).astype(BF16))
        q_sub.append(jnp.where(lane < DIFF_QK_DIM, 0.0, q).astype(BF16))
    n_streams = len(q_sub)

    meta_keep = lax.broadcasted_iota(jnp.int32, (tq, CHUNK), 1) >= meta_pad
    for n in range(n_streams):
        s = jnp.where(meta_keep, _dot_nt(q_sub[n], km_ref[:, heads[n]:heads[n] + LANES]), -jnp.inf)
        sm_ref[n] = s
        mx_ref[n] = s

    def qk_block(j, keep):
        rows = pl.ds(pl.multiple_of(j * tq, tq), tq)
        for n in range(n_streams):
            s = _dot_nt(q_sub[n], k_ref[0, rows, heads[n]:heads[n] + LANES])
            if keep is not None:
                s = jnp.where(keep, s, -jnp.inf)
            s_ref[n, j] = s
            mx_ref[n] = jnp.maximum(mx_ref[n], _fold_lanes(s, jnp.maximum))

    def qk_body(j, carry):
        qk_block(j, None)
        return carry

    lax.fori_loop(0, i, qk_body, 0)
    qk_block(i, lax.broadcasted_iota(jnp.int32, (tq, tq), 1) <= lax.broadcasted_iota(jnp.int32, (tq, tq), 0))

    for n in range(n_streams):
        mx_ref[n] = jnp.broadcast_to(jnp.max(mx_ref[n], axis=-1, keepdims=True), (tq, LANES))

    ones_meta = jnp.ones((CHUNK, LANES), BF16)
    for n in range(n_streams):
        p = jnp.exp2(sm_ref[n] - mx_ref[n])
        v_meta = jnp.concatenate([vm_ref[:, heads[n]:heads[n] + LANES], ones_meta], axis=1)
        acc_ref[n] = _dot(p.astype(BF16), v_meta)

    ones_blk = jnp.ones((tq, LANES), BF16)

    def pv_body(j, carry):
        rows = pl.ds(pl.multiple_of(j * tq, tq), tq)
        for n in range(n_streams):
            vb = jnp.concatenate([v_ref[0, rows, heads[n]:heads[n] + LANES], ones_blk], axis=1)
            p = _exp2_minus(s_ref[n, j], mx_ref[n])
            acc_ref[n] = acc_ref[n] + _dot(p.astype(BF16), vb)
        return carry

    lax.fori_loop(0, i + 1, pv_body, 0)

    lam = _diff_lambda(lam_ref, lambda_init)
    for hh in range(ATT_HEADS):
        acc0, acc1 = acc_ref[2 * hh], acc_ref[2 * hh + 1]
        o = acc0[:, 0:DIFF_V_DIM] / acc0[:, DIFF_V_DIM:] - lam * (acc1[:, 0:DIFF_V_DIM] / acc1[:, DIFF_V_DIM:])
        ms = jnp.mean(o * o, axis=-1, keepdims=True)
        o = o * lax.rsqrt(ms + NORM_EPS) * sw_ref[...] * (1.0 - lambda_init)
        o_ref[0, :, hh * LANES:(hh + 1) * LANES] = o.astype(o_ref.dtype)


def _attn_prompt(q3, k3, v3, k_meta, v_meta, lam_vecs, subln_w, lambda_init, meta_pad):
    nb, seq_len, _ = q3.shape
    n_q = seq_len // ATT_TQ
    kern = functools.partial(_attn_prompt_kernel, lambda_init=lambda_init, meta_pad=meta_pad)
    width = ATT_HEADS * LANES
    n_streams = 2 * ATT_HEADS
    q_spec = pl.BlockSpec((1, ATT_TQ, width), lambda b, h, i: (b, i, h))
    kv_spec = pl.BlockSpec((1, seq_len, width), lambda b, h, i: (b, 0, h))
    meta_spec = pl.BlockSpec((CHUNK, width), lambda b, h, i: (0, h))
    return pl.pallas_call(
        kern,
        grid=(nb, DIFF_HEADS // ATT_HEADS, n_q),
        in_specs=[q_spec, kv_spec, kv_spec, meta_spec, meta_spec,
                  _const_spec(lam_vecs.shape), _const_spec(subln_w.shape)],
        out_specs=q_spec,
        out_shape=jax.ShapeDtypeStruct((nb, seq_len, DIFF_HEADS * DIFF_V_DIM), BF16),
        scratch_shapes=[pltpu.VMEM((n_streams, n_q, ATT_TQ, ATT_TQ), F32),
                        pltpu.VMEM((n_streams, ATT_TQ, CHUNK), F32),
                        pltpu.VMEM((n_streams, ATT_TQ, LANES), F32),
                        pltpu.VMEM((n_streams, ATT_TQ, DIFF_V_DIM + LANES), F32)],
        compiler_params=_cparams(("parallel", "parallel", "arbitrary")),
        name="attn_prompt",
    )(q3, k3, v3, k_meta, v_meta, lam_vecs, subln_w)


N_SUB = 2 * DIFF_HEADS


def _decode_attn_init(j, scratch):
    m_ref, l_ref, acc_ref = scratch

    @pl.when(j == 0)
    def _():
        m_ref[...] = jnp.full(m_ref.shape, -jnp.inf, F32)
        l_ref[...] = jnp.zeros(l_ref.shape, F32)
        acc_ref[...] = jnp.zeros(acc_ref.shape, F32)


def _decode_attn_step(j, n_steps, npg, refs, o_ref, scratch, lambda_init):
    k_refs = refs[0:npg]
    v_refs = refs[npg:2 * npg]
    q_ref, kn_ref, vn_ref, rep_ref, lam_ref, sw_ref = refs[2 * npg:]
    m_ref, l_ref, acc_ref = scratch

    row = lax.broadcasted_iota(jnp.int32, (N_SUB, HQK), 0)
    col = lax.broadcasted_iota(jnp.int32, (N_SUB, HQK), 1)
    row_head = row % DIFF_HEADS
    q_mask = (col // DIFF_QK_DIM) == (2 * row_head + row // DIFF_HEADS)
    p_mask = (col % DIFF_HEADS) == row_head
    q_bd = jnp.where(q_mask, q_ref[0], 0.0)
    q_bdb = q_bd.astype(BF16)

    s = [_dot(q_bdb, k_refs[u][0].astype(BF16)) for u in range(npg)]
    m_old = m_ref[...]
    m_new = m_old
    for u in range(npg):
        m_new = jnp.maximum(m_new, jnp.max(s[u], axis=-1, keepdims=True))
    alpha = jnp.exp2(m_old - m_new)
    p = [jnp.exp2(s[u] - m_new) for u in range(npg)]
    l_new = alpha * l_ref[...]
    for u in range(npg):
        l_new = l_new + jnp.sum(p[u], axis=-1, keepdims=True)
    p_rep = _dot(jnp.concatenate(p, axis=0).astype(BF16), rep_ref[...])
    acc = alpha * acc_ref[...]
    for u in range(npg):
        p_exp = jnp.where(p_mask, p_rep[u * N_SUB:(u + 1) * N_SUB], 0.0).astype(BF16)
        acc = acc + _dot(p_exp, v_refs[u][0].astype(BF16))
    m_ref[...] = m_new
    l_ref[...] = l_new
    acc_ref[...] = acc

    @pl.when(j == n_steps - 1)
    def _():
        s_new = jnp.sum(q_bd * kn_ref[0], axis=-1, keepdims=True)
        m_fin = jnp.maximum(m_new, s_new)
        a_fin = jnp.exp2(m_new - m_fin)
        p_new = jnp.exp2(s_new - m_fin)
        l_fin = a_fin * l_new + p_new
        v_new = jnp.concatenate([vn_ref[0], vn_ref[0]], axis=0)
        out = (a_fin * acc + p_new * v_new) / l_fin
        lam = _diff_lambda(lam_ref, lambda_init)
        o = out[0:DIFF_HEADS] - lam * out[DIFF_HEADS:N_SUB]
        ms = jnp.mean(o * o, axis=-1, keepdims=True)
        o_ref[0] = (o * lax.rsqrt(ms + NORM_EPS) * sw_ref[...] * (1.0 - lambda_init)).astype(o_ref.dtype)


def _fused_call(host_kernel, name, grid, semantics, args, in_specs, out_specs, out_shape, scratch_shapes=(),
                dec=None):
    if dec is None:
        return pl.pallas_call(host_kernel, grid=grid, in_specs=in_specs, out_specs=out_specs, out_shape=out_shape,
                              scratch_shapes=list(scratch_shapes), compiler_params=_cparams(semantics),
                              name=name)(*args)
    page_table, k_pages, v_pages, q, k_new, v_new, lam_vecs, subln_w, lambda_init = dec
    n_seq, n_pages = page_table.shape
    n_steps = math.prod(grid)
    spq = n_steps // n_seq
    npg = n_pages // spq
    assert spq * n_seq == n_steps and npg * spq == n_pages and v_pages.shape[1:] == k_pages.shape[1:]
    page_blk = (1,) + k_pages.shape[1:]

    def step_of(ids):
        t = ids[0]
        for size, i in zip(grid[1:], ids[1:]):
            t = t * size + i
        return t

    def page_spec(u):
        def index_map(*ids_pt):
            t, pt = step_of(ids_pt[:-1]), ids_pt[-1]
            return (pt[t // spq, (t % spq) * npg + u], 0, 0)
        return pl.BlockSpec(page_blk, index_map)

    per_seq = lambda *ids_pt: (step_of(ids_pt[:-1]) // spq, 0, 0)
    const2 = lambda *_: (0, 0)
    rep = (jnp.arange(PAGE_SIZE)[:, None] == jnp.arange(PAGE_SIZE * DIFF_HEADS)[None, :] // DIFF_HEADS).astype(BF16)
    dec_specs = [page_spec(u) for u in range(npg)] + [page_spec(u) for u in range(npg)] + [
        pl.BlockSpec((1, 1, HQK), per_seq), pl.BlockSpec((1, 1, HQK), per_seq),
        pl.BlockSpec((1, DIFF_HEADS, DIFF_V_DIM), per_seq),
        pl.BlockSpec(rep.shape, const2), pl.BlockSpec(lam_vecs.shape, const2), pl.BlockSpec(subln_w.shape, const2)]
    dec_args = [k_pages] * npg + [v_pages] * npg + [q, k_new, v_new, rep, lam_vecs, subln_w]
    n_in, n_out, n_scr, n_dec = len(in_specs), len(out_specs), len(scratch_shapes), len(dec_specs)

    def kern(pt_ref, *refs):
        ins, dec_in = refs[:n_in], refs[n_in:n_in + n_dec]
        outs = refs[n_in + n_dec:n_in + n_dec + n_out]
        o_dec = refs[n_in + n_dec + n_out]
        scr = refs[n_in + n_dec + n_out + 1:n_in + n_dec + n_out + 1 + n_scr]
        dec_scr = refs[n_in + n_dec + n_out + 1 + n_scr:]
        j = step_of([pl.program_id(a) for a in range(len(grid))]) % spq
        _decode_attn_init(j, dec_scr)
        host_kernel(*ins, *outs, *scr)
        _decode_attn_step(j, spq, npg, dec_in, o_dec, dec_scr, lambda_init)

    grid_spec = pltpu.PrefetchScalarGridSpec(
        num_scalar_prefetch=1,
        grid=grid,
        in_specs=list(in_specs) + dec_specs,
        out_specs=list(out_specs) + [pl.BlockSpec((1, DIFF_HEADS, DIFF_V_DIM), per_seq)],
        scratch_shapes=list(scratch_shapes) + [pltpu.VMEM((N_SUB, 1), F32), pltpu.VMEM((N_SUB, 1), F32),
                                               pltpu.VMEM((N_SUB, DIFF_V_DIM), F32)],
    )
    return pl.pallas_call(
        kern,
        grid_spec=grid_spec,
        out_shape=list(out_shape) + [jax.ShapeDtypeStruct((n_seq, DIFF_HEADS, DIFF_V_DIM), BF16)],
        compiler_params=_cparams(("arbitrary",) * len(grid)),
        name=name + "_dec",
    )(page_table, *args, *dec_args)


def _row_tiled_call(host_kernel, name, args, in_specs, out_specs, out_shape, n_steps):
    return _fused_call(host_kernel, name, (n_steps,), ("parallel",), args, in_specs, out_specs, out_shape)


def kernel(x_prompt, x_sample, cache_k, cache_v, state_ssm, state_conv, page_table, meta_tokens, a_w_in, a_conv_w, a_conv_b, a_dt_bias, a_A_log, a_D, a_norm_w, a_w_out, kv_w_k, kv_w_v, b_w_q, b_lambda, b_subln_w, b_w_o, ffn_w_gate, ffn_w_up, ffn_w_down, ln_mix_w, ln_mix_b, ln_ffn_w, ln_ffn_b):
    bp, seq, _ = x_prompt.shape
    bs = x_sample.shape[0]
    n_pages = page_table.shape[1]
    past_len = n_pages * PAGE_SIZE
    tm_p = ROW_TILE
    lambda_init = 0.8 - 0.6 * math.exp(-0.3 * 1)
    scale = DIFF_QK_DIM ** -0.5

    w_in = a_w_in[0]
    wz = w_in[:, :D_INNER].astype(BF16)
    wx = w_in[:, D_INNER:D_INNER + CONV_DIM].astype(BF16)
    wdt = jnp.pad(w_in[:, D_INNER + CONV_DIM:], ((0, 0), (0, LANES - SSM_HEADS))).astype(BF16)
    pad_h = lambda v: jnp.pad(v.reshape(1, SSM_HEADS), ((0, 0), (0, LANES - SSM_HEADS)))
    dt_bias, a_log = pad_h(a_dt_bias[0]), pad_h(a_A_log[0])
    conv_w, conv_b = a_conv_w[0], a_conv_b[0].reshape(1, CONV_DIM)
    d_exp = jnp.repeat(a_D[0], SSM_HEAD_DIM).reshape(1, D_INNER)
    norm_w = a_norm_w[0].reshape(1, D_INNER)
    w_out = a_w_out[0].astype(BF16)
    wk, wv = kv_w_k.astype(BF16), kv_w_v.astype(BF16)
    wq = (b_w_q[0] * (scale * LOG2_E)).astype(BF16)
    wo = b_w_o[0].astype(BF16)
    wg, wu, wd = ffn_w_gate.astype(BF16), ffn_w_up.astype(BF16), ffn_w_down.astype(BF16)
    ln = lambda t, l: t[l].reshape(1, D_MODEL)
    subln_w = b_subln_w[0].reshape(1, DIFF_V_DIM)
    lam_vecs = b_lambda[0]

    n_main = bp * seq
    n_aux = CHUNK + bs
    hm = x_prompt.reshape(n_main, D_MODEL)
    ha = jnp.concatenate([jnp.zeros((PAD_ROWS, D_MODEL), F32), meta_tokens, x_sample.reshape(bs, D_MODEL)], axis=0)
    smp = lambda t: t[CHUNK:]

    z_a, xbc_a, dtr_a = _in_proj(ha, wz, wx, wdt, n_aux)
    y_meta, tail_meta, _, st_meta = _mamba_prompt(
        xbc_a, z_a, dtr_a, jnp.zeros((SUBLANES, CONV_DIM), F32), jnp.zeros((SSM_STATE, D_INNER), F32),
        conv_w, conv_b, dt_bias, a_log, d_exp, norm_w, 1, 1, PAD_ROWS)
    prev = jnp.transpose(state_conv[0], (1, 0, 2))
    xc_s, conv_s, dt_s, decay_s = _conv_step(smp(xbc_a), prev, conv_w, conv_b, smp(dtr_a), dt_bias, a_log)
    per_ch = lambda t: jnp.repeat(t[:, :SSM_HEADS], SSM_HEAD_DIM, axis=1)
    y_s, ssm_s = _ssd_step(xc_s[:, :D_INNER], smp(z_a), per_ch(dt_s), per_ch(decay_s),
                           xc_s[:, D_INNER:D_INNER + GN].reshape(bs, SSM_GROUPS, SSM_STATE),
                           xc_s[:, D_INNER + GN:].reshape(bs, SSM_GROUPS, SSM_STATE),
                           state_ssm[0], d_exp, norm_w)
    y_a = jnp.concatenate([y_meta, y_s], axis=0)

    ha = _proj_ffn_ln(y_a, ha, w_out, ln(ln_mix_w, 0), ln(ln_mix_b, 0),
                      wg[0], wu[0], wd[0], ln(ln_ffn_w, 0), ln(ln_ffn_b, 0), n_aux)
    pos_a = jnp.concatenate([jnp.zeros((PAD_ROWS,), jnp.int32), jnp.arange(N_META),
                             jnp.full((bs,), past_len, jnp.int32)])
    k_a, v_a, kb_a, vb_a, q_a = _kvq(ha, wk, wv, wq, *_rope_tables(pos_a), n_aux, F32)
    hs, k_s, v_s, q_s = smp(ha), smp(k_a), smp(v_a), smp(q_a)

    n_pool = cache_k.shape[0]
    k_pages = jnp.transpose(cache_k, (0, 2, 3, 4, 1)).reshape(n_pool, HQK, PAGE_SIZE)
    v_pages = cache_v.reshape(n_pool, PAGE_SIZE * DIFF_HEADS, DIFF_V_DIM)
    dec = (page_table, k_pages, v_pages, q_s.reshape(bs, 1, HQK), k_s.reshape(bs, 1, HQK),
           v_s.reshape(bs, DIFF_HEADS, DIFF_V_DIM), lam_vecs, subln_w, lambda_init)
    z_m, xbc_m, dtr_m = _in_proj(hm, wz, wx, wdt, tm_p)
    y_m, conv_tail, ssm_p, _, o_s = _mamba_prompt(xbc_m, z_m, dtr_m, tail_meta[0], st_meta[0], conv_w, conv_b,
                                                  dt_bias, a_log, d_exp, norm_w, bp, seq // CHUNK, 0, dec)
    hm = _proj_ffn_ln(y_m, hm, w_out, ln(ln_mix_w, 0), ln(ln_mix_b, 0),
                      wg[0], wu[0], wd[0], ln(ln_ffn_w, 0), ln(ln_ffn_b, 0), tm_p)
    k_m, v_m, kb_m, vb_m, qb_m = _kvq(hm, wk, wv, wq, *_rope_tables(N_META + jnp.arange(seq)), tm_p, BF16)

    to3 = lambda t: t.reshape(bp, seq, D_MODEL)
    o_m = _attn_prompt(to3(qb_m), to3(kb_m), to3(vb_m), kb_a, vb_a, lam_vecs, subln_w, lambda_init, PAD_ROWS)
    hm = _proj_ffn_ln(o_m.reshape(n_main, D_MODEL), hm, wo, ln(ln_mix_w, 1), ln(ln_mix_b, 1),
                      wg[1], wu[1], wd[1], ln(ln_ffn_w, 1), ln(ln_ffn_b, 1), tm_p)

    hs = _proj_ffn_ln(o_s.reshape(bs, D_MODEL), hs, wo, ln(ln_mix_w, 1), ln(ln_mix_b, 1),
                      wg[1], wu[1], wd[1], ln(ln_ffn_w, 1), ln(ln_ffn_b, 1), bs)

    with_meta = lambda meta, main: jnp.concatenate(
        [jnp.broadcast_to(meta[None, PAD_ROWS:CHUNK], (bp, N_META, D_MODEL)), to3(main)], axis=1)
    y_prompt = to3(hm)
    k_prompt = with_meta(k_a, k_m).reshape(bp, N_META + seq, DIFF_HEADS, 2, DIFF_QK_DIM)
    v_prompt = with_meta(v_a, v_m).reshape(bp, N_META + seq, DIFF_HEADS, DIFF_V_DIM)
    ssm_prompt = ssm_p[None]
    conv_prompt = conv_tail[None, :, SUBLANES - (D_CONV - 1):]
    y_sample = hs.reshape(bs, 1, D_MODEL)
    k_sample = k_s.reshape(bs, 1, DIFF_HEADS, 2, DIFF_QK_DIM)
    v_sample = v_s.reshape(bs, 1, DIFF_HEADS, DIFF_V_DIM)
    ssm_sample = ssm_s[None]
    conv_sample = jnp.transpose(conv_s, (1, 0, 2))[None]
    return (y_prompt, y_sample, k_prompt, v_prompt, ssm_prompt, conv_prompt, k_sample, v_sample, ssm_sample,
            conv_sample)
```

```python
import functools
import math

import jax
import jax.numpy as jnp
from jax import lax
from jax.experimental import pallas as pl
from jax.experimental.pallas import tpu as pltpu

F32 = jnp.float32
BF16 = jnp.bfloat16

D_MODEL = 1024
N_META = 16
D_INNER = 2048
SSM_HEADS = 32
SSM_HEAD_DIM = 64
SSM_GROUPS = 4
SSM_STATE = 128
D_CONV = 4
GN = SSM_GROUPS * SSM_STATE
CONV_DIM = D_INNER + 2 * GN
DIFF_HEADS = 8
DIFF_QK_DIM = 64
DIFF_V_DIM = 128
HQK = DIFF_HEADS * 2 * DIFF_QK_DIM
ROT_DIM = DIFF_QK_DIM // 4
ROPE_THETA = 500000.0
D_FF = 2816
DEPTH = 2
DEEPNORM_ALPHA = (2 * DEPTH) ** 0.25
NORM_EPS = 1e-5
PAGE_SIZE = 128
LOG2_E = math.log2(math.e)

LANES = 128
SUBLANES = 8
VMEM_LIMIT_BYTES = 56 * 1024 * 1024

CHUNK = 128
CONV_COLS = 512
PAD_ROWS = CHUNK - N_META
ATT_TQ = 512
ATT_HEADS = 2
DEC_PAGES_PER_STEP = 8
ROW_TILE = 512


def _cparams(sem):
    return pltpu.CompilerParams(dimension_semantics=sem, vmem_limit_bytes=VMEM_LIMIT_BYTES)


def _const_spec(shape):
    nd = len(shape)
    return pl.BlockSpec(shape, lambda *_: (0,) * nd)


def _weight_spec(shape):
    nd = len(shape)
    return pl.BlockSpec(shape, lambda *_: (0,) * nd, pipeline_mode=pl.Buffered(1))


def _ROW(i, *_):
    return (i, 0)


def _layer_norm(x, w, b):
    mu = jnp.mean(x, axis=-1, keepdims=True)
    xc = x - mu
    var = jnp.mean(xc * xc, axis=-1, keepdims=True)
    return xc * lax.rsqrt(var + NORM_EPS) * w + b


def _silu(x):
    h = 0.5 * x
    return h + h * jnp.tanh(h)


def _softplus(x):
    return jnp.maximum(x, 0.0) + jnp.log1p(jnp.exp(-jnp.abs(x)))


def _dot(a, b):
    return jnp.dot(a, b, preferred_element_type=F32)


def _dot_nt(a, b):
    return lax.dot_general(a, b, (((1,), (1,)), ((), ())), preferred_element_type=F32)


NCHUNK = 512


def _in_proj_kernel(x_ref, wz_ref, wx_ref, wdt_ref, z_ref, xbc_ref, dt_ref):
    xb = x_ref[...].astype(BF16)
    for n0 in range(0, D_INNER, NCHUNK):
        z_ref[:, n0:n0 + NCHUNK] = _dot(xb, wz_ref[:, n0:n0 + NCHUNK])
    for n0 in range(0, CONV_DIM, NCHUNK):
        xbc_ref[:, n0:n0 + NCHUNK] = _dot(xb, wx_ref[:, n0:n0 + NCHUNK])
    dt_ref[...] = _dot(xb, wdt_ref[...])


def _in_proj(x, wz, wx, wdt, tm):
    m = x.shape[0]
    return _row_tiled_call(
        _in_proj_kernel, "in_proj", (x, wz, wx, wdt),
        in_specs=[pl.BlockSpec((tm, D_MODEL), _ROW), _weight_spec(wz.shape), _weight_spec(wx.shape),
                  _weight_spec(wdt.shape)],
        out_specs=[pl.BlockSpec((tm, D_INNER), _ROW), pl.BlockSpec((tm, CONV_DIM), _ROW),
                   pl.BlockSpec((tm, LANES), _ROW)],
        out_shape=[jax.ShapeDtypeStruct((m, D_INNER), F32), jax.ShapeDtypeStruct((m, CONV_DIM), F32),
                   jax.ShapeDtypeStruct((m, LANES), F32)],
        n_steps=m // tm)


def _causal_conv_silu(ext, cw_ref, cb_ref, cols):
    prev1 = pltpu.roll(ext, 1, 0)
    u = cw_ref[1:2, cols] * ext + cw_ref[0:1, cols] * prev1
    conv = (cb_ref[:, cols] + cw_ref[3:4, cols] * ext[SUBLANES:] + cw_ref[2:3, cols] * prev1[SUBLANES:]
            + pltpu.roll(u, 2, 0)[SUBLANES:])
    return _silu(conv)


def _in_proj_conv_kernel(x_ref, xh_ref, tail0_ref, wz_ref, wx_ref, wdt_ref, cw_ref, cb_ref,
                         z_ref, xc_ref, dt_ref, raw_tail_ref, *, tiles_per_seq):
    tm = x_ref.shape[0]
    seq_start = (pl.program_id(0) % tiles_per_seq) == 0
    xb = x_ref[...].astype(BF16)
    xhb = xh_ref[...].astype(BF16)
    for n0 in range(0, D_INNER, NCHUNK):
        z_ref[:, n0:n0 + NCHUNK] = _dot(xb, wz_ref[:, n0:n0 + NCHUNK])
    for n0 in range(0, CONV_DIM, NCHUNK):
        cols = slice(n0, n0 + NCHUNK)
        raw = _dot(xb, wx_ref[:, cols])
        halo = jnp.where(seq_start, tail0_ref[:, cols], _dot(xhb, wx_ref[:, cols]))
        xc_ref[:, cols] = _causal_conv_silu(jnp.concatenate([halo, raw], axis=0), cw_ref, cb_ref, cols)
        raw_tail_ref[0, :, cols] = raw[tm - SUBLANES:tm]
    dt_ref[...] = _dot(xb, wdt_ref[...])


def _in_proj_conv(x, tail0, wz, wx, wdt, conv_w, conv_b, tm, seq):
    m = x.shape[0]
    n_tiles = m // tm
    halo_blocks = tm // SUBLANES
    return _row_tiled_call(
        functools.partial(_in_proj_conv_kernel, tiles_per_seq=seq // tm), "in_proj_conv",
        (x, x, tail0, wz, wx, wdt, conv_w, conv_b),
        in_specs=[pl.BlockSpec((tm, D_MODEL), _ROW),
                  pl.BlockSpec((SUBLANES, D_MODEL), lambda i, *_: (jnp.maximum(i * halo_blocks - 1, 0), 0)),
                  _const_spec(tail0.shape), _weight_spec(wz.shape), _weight_spec(wx.shape), _weight_spec(wdt.shape),
                  _const_spec(conv_w.shape), _const_spec(conv_b.shape)],
        out_specs=[pl.BlockSpec((tm, D_INNER), _ROW), pl.BlockSpec((tm, CONV_DIM), _ROW),
                   pl.BlockSpec((tm, LANES), _ROW), pl.BlockSpec((1, SUBLANES, CONV_DIM), lambda i, *_: (i, 0, 0))],
        out_shape=[jax.ShapeDtypeStruct((m, D_INNER), F32), jax.ShapeDtypeStruct((m, CONV_DIM), F32),
                   jax.ShapeDtypeStruct((m, LANES), F32), jax.ShapeDtypeStruct((n_tiles, SUBLANES, CONV_DIM), F32)],
        n_steps=n_tiles)


def _mamba_prompt_kernel(xbc_ref, z_ref, dtr_ref, tail0_ref, st0_ref, cw_ref, cb_ref, dtb_ref, alog_ref, dexp_ref,
                         nw_ref, y_ref, conv_out_ref, ssm_out_ref, st_out_ref,
                         ext_ref, st_ref, xc_scr_ref, ybuf_ref, *, pad_rows, conv_done):
    c = pl.program_id(1)
    n_chunks = pl.num_programs(1)

    @pl.when(c == 0)
    def _():
        ext_ref[0:SUBLANES, :] = tail0_ref[...]
        st_ref[...] = st0_ref[...]

    if conv_done:
        xc_ref = xbc_ref
    else:
        xc_ref = xc_scr_ref
        x_raw = xbc_ref[...]
        ext_ref[SUBLANES:SUBLANES + CHUNK, :] = x_raw
        for n0 in range(0, CONV_DIM, CONV_COLS):
            cols = slice(n0, n0 + CONV_COLS)
            xc_ref[:, cols] = _causal_conv_silu(ext_ref[:, cols], cw_ref, cb_ref, cols)
        ext_ref[0:SUBLANES, :] = x_raw[CHUNK - SUBLANES:CHUNK, :]

    row = lax.broadcasted_iota(jnp.int32, (CHUNK, LANES), 0)
    col = lax.broadcasted_iota(jnp.int32, (CHUNK, LANES), 1)
    dt = _softplus(dtr_ref[...] + dtb_ref[...])
    if pad_rows:
        dt = jnp.where(jnp.logical_or(c > 0, row >= pad_rows), dt, 0.0)
    a = dt * (-jnp.exp(alog_ref[...]))
    tri = (row >= col).astype(F32)
    cs = jnp.dot(tri, a, preferred_element_type=F32, precision=lax.Precision.HIGHEST)
    cs_t = cs.T
    dt_t = dt.T
    e_cs = jnp.exp(cs)
    causal = row >= col
    lane_lo = col < SSM_HEAD_DIM

    for g in range(SSM_GROUPS):
        b_g = xc_ref[:, D_INNER + g * SSM_STATE:D_INNER + (g + 1) * SSM_STATE]
        c_g = xc_ref[:, D_INNER + GN + g * SSM_STATE:D_INNER + GN + (g + 1) * SSM_STATE]
        c_gb = c_g.astype(BF16)
        b_gb = b_g.astype(BF16)
        cb = _dot_nt(c_gb, b_gb)
        bt = b_g.T
        g0 = g * (D_INNER // SSM_GROUPS)
        y_off_g = _dot(c_gb, st_ref[:, g0:g0 + D_INNER // SSM_GROUPS].astype(BF16))
        for jp in range(D_INNER // SSM_GROUPS // LANES):
            lo = g0 + jp * LANES
            h0 = lo // SSM_HEAD_DIM
            m_parts, mp_parts, dec_parts, ecol_parts = [], [], [], []
            for h in (h0, h0 + 1):
                colv = cs[:, h:h + 1]
                rowv = cs_t[h:h + 1, :]
                dtrow = dt_t[h:h + 1, :]
                last = cs[CHUNK - 1:CHUNK, h:h + 1]
                lmat = jnp.exp(jnp.where(causal, colv - rowv, -jnp.inf))
                m_parts.append((cb * lmat * dtrow).astype(BF16))
                mp_parts.append((bt * (jnp.exp(last - rowv) * dtrow)).astype(BF16))
                dec_parts.append(jnp.exp(last))
                ecol_parts.append(e_cs[:, h:h + 1])
            lhs = jnp.concatenate([jnp.concatenate(m_parts, axis=1),
                                   jnp.concatenate(mp_parts, axis=1)], axis=0)
            x_pair = xc_ref[:, lo:lo + LANES]
            rhs = jnp.concatenate([jnp.where(lane_lo, x_pair, 0.0).astype(BF16),
                                   jnp.where(lane_lo, 0.0, x_pair).astype(BF16)], axis=0)
            res = _dot(lhs, rhs)
            y_diag = res[0:CHUNK]
            d_state = res[CHUNK:2 * CHUNK]
            decay = jnp.where(lane_lo, dec_parts[0], dec_parts[1])
            ecol = jnp.where(lane_lo, ecol_parts[0], ecol_parts[1])
            st_ref[:, lo:lo + LANES] = st_ref[:, lo:lo + LANES] * decay + d_state
            y_pair = y_diag + y_off_g[:, jp * LANES:(jp + 1) * LANES] * ecol + x_pair * dexp_ref[:, lo:lo + LANES]
            ybuf_ref[:, lo:lo + LANES] = y_pair

    gated = ybuf_ref[...] * _silu(z_ref[...])
    ms = jnp.mean(gated * gated, axis=-1, keepdims=True)
    y_ref[...] = (gated * lax.rsqrt(ms + NORM_EPS) * nw_ref[...]).astype(y_ref.dtype)

    @pl.when(c == n_chunks - 1)
    def _():
        if conv_done:
            conv_out_ref[...] = jnp.zeros(conv_out_ref.shape, F32)
        else:
            conv_out_ref[0] = xbc_ref[CHUNK - SUBLANES:CHUNK, :]
        st_out_ref[0] = st_ref[...]
        for jp in range(D_INNER // LANES):
            t = st_ref[:, jp * LANES:(jp + 1) * LANES].T
            ssm_out_ref[0, 2 * jp] = t[0:SSM_HEAD_DIM]
            ssm_out_ref[0, 2 * jp + 1] = t[SSM_HEAD_DIM:2 * SSM_HEAD_DIM]


def _mamba_prompt(xbc, z, dt_raw, tail0, st0, conv_w, conv_b, dt_bias, a_log, d_exp, norm_w, nb, n_chunks, pad_rows,
                  conv_done=False, dec=None):
    m = nb * n_chunks * CHUNK
    blk = lambda b, c, *_: (b * n_chunks + c, 0)
    per_seq3 = lambda b, c, *_: (b, 0, 0)
    return _fused_call(
        functools.partial(_mamba_prompt_kernel, pad_rows=pad_rows, conv_done=conv_done), "mamba_prompt",
        (nb, n_chunks),
        ("parallel", "arbitrary"), (xbc, z, dt_raw, tail0, st0, conv_w, conv_b, dt_bias, a_log, d_exp, norm_w),
        in_specs=[pl.BlockSpec((CHUNK, CONV_DIM), blk), pl.BlockSpec((CHUNK, D_INNER), blk),
                  pl.BlockSpec((CHUNK, LANES), blk), _const_spec(tail0.shape), _const_spec(st0.shape),
                  _const_spec(conv_w.shape), _const_spec(conv_b.shape), _const_spec(dt_bias.shape),
                  _const_spec(a_log.shape), _const_spec(d_exp.shape), _const_spec(norm_w.shape)],
        out_specs=[pl.BlockSpec((CHUNK, D_INNER), blk),
                   pl.BlockSpec((1, SUBLANES, CONV_DIM), per_seq3),
                   pl.BlockSpec((1, SSM_HEADS, SSM_HEAD_DIM, SSM_STATE), lambda b, c, *_: (b, 0, 0, 0)),
                   pl.BlockSpec((1, SSM_STATE, D_INNER), per_seq3)],
        out_shape=[jax.ShapeDtypeStruct((m, D_INNER), BF16),
                   jax.ShapeDtypeStruct((nb, SUBLANES, CONV_DIM), F32),
                   jax.ShapeDtypeStruct((nb, SSM_HEADS, SSM_HEAD_DIM, SSM_STATE), F32),
                   jax.ShapeDtypeStruct((nb, SSM_STATE, D_INNER), F32)],
        scratch_shapes=[pltpu.VMEM((SUBLANES + CHUNK, CONV_DIM), F32),
                        pltpu.VMEM((SSM_STATE, D_INNER), F32),
                        pltpu.VMEM((CHUNK, CONV_DIM), F32),
                        pltpu.VMEM((CHUNK, D_INNER), F32)],
        dec=dec)


def _conv_step_kernel(xbc_ref, prev_ref, cw_ref, cb_ref, dtr_ref, dtb_ref, alog_ref,
                      xc_ref, newconv_ref, dt_ref, decay_ref):
    x_raw = xbc_ref[...]
    conv = cb_ref[...] + cw_ref[3:4, :] * x_raw
    for k in range(D_CONV - 1):
        conv = conv + cw_ref[k:k + 1, :] * prev_ref[k]
    xc_ref[...] = _silu(conv)
    newconv_ref[0] = prev_ref[1]
    newconv_ref[1] = prev_ref[2]
    newconv_ref[2] = x_raw
    dt = _softplus(dtr_ref[...] + dtb_ref[...])
    dt_ref[...] = dt
    decay_ref[...] = jnp.exp(dt * (-jnp.exp(alog_ref[...])))


def _conv_step(xbc, prev, conv_w, conv_b, dt_raw, dt_bias, a_log):
    nb = xbc.shape[0]
    return pl.pallas_call(
        _conv_step_kernel,
        out_shape=[jax.ShapeDtypeStruct((nb, CONV_DIM), F32), jax.ShapeDtypeStruct((D_CONV - 1, nb, CONV_DIM), F32),
                   jax.ShapeDtypeStruct((nb, LANES), F32), jax.ShapeDtypeStruct((nb, LANES), F32)],
        compiler_params=pltpu.CompilerParams(vmem_limit_bytes=VMEM_LIMIT_BYTES),
        name="conv_step",
    )(xbc, prev, conv_w, conv_b, dt_raw, dt_bias, a_log)


N_PAIRS = D_INNER // LANES


def _ssd_step_kernel(x_ref, z_ref, dt_ref, dec_ref, b_ref, c_ref, h_ref, dsk_ref, nw_ref, y_ref, hout_ref):
    x = x_ref[0]
    pad = jnp.zeros((LANES - N_PAIRS, LANES), F32)
    cols = lambda v: jnp.concatenate([v, pad], axis=0).T
    x_t, dtx_t, dec_t = cols(x), cols(dt_ref[0] * x), cols(dec_ref[0])
    lane = lax.broadcasted_iota(jnp.int32, (LANES, LANES), 1)
    y_t = jnp.zeros((LANES, LANES), F32)
    for r in range(N_PAIRS):
        g = r // (N_PAIRS // SSM_GROUPS)
        b_row = b_ref[0, g:g + 1, :]
        c_row = c_ref[0, g:g + 1, :]
        dtx_col, dec_col = dtx_t[:, r:r + 1], dec_t[:, r:r + 1]
        h0 = h_ref[0, r]
        cb = jnp.sum(c_row * b_row, axis=-1, keepdims=True)
        y_col = dec_col * jnp.sum(h0 * c_row, axis=-1, keepdims=True) + cb * dtx_col
        hout_ref[0, r] = dec_col * h0 + dtx_col * b_row
        y_t = jnp.where(lane == r, y_col, y_t)
    y = y_t.T[0:N_PAIRS] + x * dsk_ref[...]
    gated = y * _silu(z_ref[0])
    ms = jnp.sum(jnp.sum(gated * gated, axis=-1, keepdims=True), axis=0, keepdims=True) / D_INNER
    y_ref[0] = (gated * lax.rsqrt(ms + NORM_EPS) * nw_ref[...]).astype(y_ref.dtype)


def _ssd_step(x, z, dt_ch, decay_ch, b3, c3, h0, d_skip, norm_w):
    nb = x.shape[0]
    per_b = lambda b: (b, 0, 0)
    vec = pl.BlockSpec((1, N_PAIRS, LANES), per_b)
    pairs = lambda t: t.reshape(-1, N_PAIRS, LANES)
    st_spec = pl.BlockSpec((1, N_PAIRS, LANES, SSM_STATE), lambda b: (b, 0, 0, 0))
    grp = pl.BlockSpec((1, SSM_GROUPS, SSM_STATE), per_b)
    y, h_new = pl.pallas_call(
        _ssd_step_kernel,
        grid=(nb,),
        in_specs=[vec, vec, vec, vec, grp, grp, st_spec, _const_spec((N_PAIRS, LANES)), _const_spec((N_PAIRS, LANES))],
        out_specs=[vec, st_spec],
        out_shape=[jax.ShapeDtypeStruct((nb, N_PAIRS, LANES), BF16),
                   jax.ShapeDtypeStruct((nb, N_PAIRS, LANES, SSM_STATE), F32)],
        compiler_params=_cparams(("parallel",)),
        name="ssd_step",
    )(pairs(x), pairs(z), pairs(dt_ch), pairs(decay_ch), b3, c3, h0.reshape(nb, N_PAIRS, LANES, SSM_STATE),
      d_skip.reshape(N_PAIRS, LANES), norm_w.reshape(N_PAIRS, LANES))
    return y.reshape(nb, D_INNER), h_new.reshape(h0.shape)


FF_CHUNK = 256


def _proj_ffn_ln_kernel(a_ref, r_ref, w_ref, mw_ref, mb_ref, wg_ref, wu_ref, wd_ref, fw_ref, fb_ref, o_ref):
    mix = _dot(a_ref[...], w_ref[...])
    x = _layer_norm(DEEPNORM_ALPHA * r_ref[...] + mix, mw_ref[...], mb_ref[...])
    xb = x.astype(BF16)
    acc = jnp.zeros(x.shape, F32)
    for f0 in range(0, D_FF, FF_CHUNK):
        gate = _dot(xb, wg_ref[:, f0:f0 + FF_CHUNK])
        up = _dot(xb, wu_ref[:, f0:f0 + FF_CHUNK])
        act = (_silu(gate) * up).astype(BF16)
        acc = acc + _dot(act, wd_ref[f0:f0 + FF_CHUNK, :])
    o_ref[...] = _layer_norm(DEEPNORM_ALPHA * x + acc, fw_ref[...], fb_ref[...])


def _proj_ffn_ln(a, resid, w, mix_w, mix_b, wg, wu, wd, ffn_w, ffn_b, tm):
    m, k = a.shape
    vec = _const_spec((1, D_MODEL))
    return _row_tiled_call(
        _proj_ffn_ln_kernel, "proj_ffn_ln", (a, resid, w, mix_w, mix_b, wg, wu, wd, ffn_w, ffn_b),
        in_specs=[pl.BlockSpec((tm, k), _ROW), pl.BlockSpec((tm, D_MODEL), _ROW), _weight_spec(w.shape), vec, vec,
                  _weight_spec(wg.shape), _weight_spec(wu.shape), _weight_spec(wd.shape), vec, vec],
        out_specs=[pl.BlockSpec((tm, D_MODEL), _ROW)],
        out_shape=[jax.ShapeDtypeStruct((m, D_MODEL), F32)],
        n_steps=m // tm)[0]


def _rotary(x, cos_t, sin_a, sin_b):
    half = ROT_DIM // 2
    parts = []
    for j in range(x.shape[1] // LANES):
        xb = x[:, j * LANES:(j + 1) * LANES]
        fwd = pltpu.roll(xb, LANES - half, 1)
        bwd = pltpu.roll(xb, half, 1)
        parts.append(xb * cos_t + fwd * sin_a + bwd * sin_b)
    return jnp.concatenate(parts, axis=1)


def _kvq_kernel(h_ref, wk_ref, wv_ref, wq_ref, cos_ref, sa_ref, sb_ref, k_ref, v_ref, kb_ref, vb_ref, qb_ref):
    hb = h_ref[...].astype(BF16)
    cos_t, sin_a, sin_b = cos_ref[...], sa_ref[...], sb_ref[...]
    k = _rotary(_dot(hb, wk_ref[...]), cos_t, sin_a, sin_b)
    k_ref[...] = k
    kb_ref[...] = k.astype(BF16)
    v = _dot(hb, wv_ref[...])
    v_ref[...] = v
    vb_ref[...] = v.astype(BF16)
    q = _rotary(_dot(hb, wq_ref[...]), cos_t, sin_a, sin_b)
    qb_ref[...] = q.astype(qb_ref.dtype)


def _kvq(h, wk, wv, wq, cos_t, sin_a, sin_b, tm, q_dtype):
    m = h.shape[0]
    tab_blocks = cos_t.shape[0] // tm
    full = pl.BlockSpec((tm, D_MODEL), _ROW)
    tab = pl.BlockSpec((tm, LANES), lambda i, *_: (i % tab_blocks, 0))
    return _row_tiled_call(
        _kvq_kernel, "kvq_proj", (h, wk, wv, wq, cos_t, sin_a, sin_b),
        in_specs=[full, _weight_spec(wk.shape), _weight_spec(wv.shape), _weight_spec(wq.shape), tab, tab, tab],
        out_specs=[full, full, full, full, full],
        out_shape=[jax.ShapeDtypeStruct((m, D_MODEL), F32), jax.ShapeDtypeStruct((m, D_MODEL), F32),
                   jax.ShapeDtypeStruct((m, D_MODEL), BF16), jax.ShapeDtypeStruct((m, D_MODEL), BF16),
                   jax.ShapeDtypeStruct((m, D_MODEL), q_dtype)],
        n_steps=m // tm)


def _rope_tables(pos):
    half = ROT_DIM // 2
    inv_freq = ROPE_THETA ** (-jnp.arange(half, dtype=F32) * 2.0 / ROT_DIM)
    ang = pos.astype(F32)[:, None] * inv_freq[None, :]
    cos, sin = jnp.cos(ang), jnp.sin(ang)
    n = pos.shape[0]
    rest = DIFF_QK_DIM - ROT_DIM
    cos_sub = jnp.concatenate([cos, cos, jnp.ones((n, rest), F32)], axis=1)
    sa_sub = jnp.concatenate([-sin, jnp.zeros((n, half + rest), F32)], axis=1)
    sb_sub = jnp.concatenate([jnp.zeros((n, half), F32), sin, jnp.zeros((n, rest), F32)], axis=1)
    rep = lambda t: jnp.concatenate([t, t], axis=1)
    return rep(cos_sub), rep(sa_sub), rep(sb_sub)


def _diff_lambda(lam_ref, lambda_init):
    lv = lam_ref[...]
    s1 = jnp.sum(lv[0:1] * lv[1:2], axis=-1, keepdims=True)
    s2 = jnp.sum(lv[2:3] * lv[3:4], axis=-1, keepdims=True)
    return jnp.exp(s1) - jnp.exp(s2) + lambda_init


def _fold_lanes(x, op):
    r = x[:, 0:LANES]
    for t in range(1, x.shape[1] // LANES):
        r = op(r, x[:, t * LANES:(t + 1) * LANES])
    return r


def _exp2_minus(s, m_rep):
    return jnp.concatenate([jnp.exp2(s[:, t * LANES:(t + 1) * LANES] - m_rep)
                            for t in range(s.shape[1] // LANES)], axis=1)


def _attn_prompt_kernel(q_ref, k_ref, v_ref, km_ref, vm_ref, lam_ref, sw_ref, o_ref,
                        s_ref, sm_ref, mx_ref, acc_ref, *, lambda_init, meta_pad):
    i = pl.program_id(2)
    tq = ATT_TQ
    lane = lax.broadcasted_iota(jnp.int32, (tq, LANES), 1)
    heads = [hh * LANES for hh in range(ATT_HEADS) for _ in range(2)]
    q_sub = []
    for hh in range(ATT_HEADS):
        q = q_ref[0, :, hh * LANES:(hh + 1) * LANES].astype(F32)
        q_sub.append(jnp.where(lane < DIFF_QK_DIM, q, 0.0).astype(BF16))
        q_sub.append(jnp.where(lane < DIFF_QK_DIM, 0.0, q).astype(BF16))
    n_streams = len(q_sub)

    meta_keep = lax.broadcasted_iota(jnp.int32, (tq, CHUNK), 1) >= meta_pad
    for n in range(n_streams):
        s = jnp.where(meta_keep, _dot_nt(q_sub[n], km_ref[:, heads[n]:heads[n] + LANES]), -jnp.inf)
        sm_ref[n] = s
        mx_ref[n] = s

    def qk_block(j, keep):
        rows = pl.ds(pl.multiple_of(j * tq, tq), tq)
        for n in range(n_streams):
            s = _dot_nt(q_sub[n], k_ref[0, rows, heads[n]:heads[n] + LANES])
            if keep is not None:
                s = jnp.where(keep, s, -jnp.inf)
            s_ref[n, j] = s
            mx_ref[n] = jnp.maximum(mx_ref[n], _fold_lanes(s, jnp.maximum))

    def qk_body(j, carry):
        qk_block(j, None)
        return carry

    lax.fori_loop(0, i, qk_body, 0)
    qk_block(i, lax.broadcasted_iota(jnp.int32, (tq, tq), 1) <= lax.broadcasted_iota(jnp.int32, (tq, tq), 0))

    for n in range(n_streams):
        mx_ref[n] = jnp.broadcast_to(jnp.max(mx_ref[n], axis=-1, keepdims=True), (tq, LANES))

    ones_meta = jnp.ones((CHUNK, LANES), BF16)
    for n in range(n_streams):
        p = jnp.exp2(sm_ref[n] - mx_ref[n])
        v_meta = jnp.concatenate([vm_ref[:, heads[n]:heads[n] + LANES], ones_meta], axis=1)
        acc_ref[n] = _dot(p.astype(BF16), v_meta)

    ones_blk = jnp.ones((tq, LANES), BF16)

    def pv_body(j, carry):
        rows = pl.ds(pl.multiple_of(j * tq, tq), tq)
        for n in range(n_streams):
            vb = jnp.concatenate([v_ref[0, rows, heads[n]:heads[n] + LANES], ones_blk], axis=1)
            p = _exp2_minus(s_ref[n, j], mx_ref[n])
            acc_ref[n] = acc_ref[n] + _dot(p.astype(BF16), vb)
        return carry

    lax.fori_loop(0, i + 1, pv_body, 0)

    lam = _diff_lambda(lam_ref, lambda_init)
    for hh in range(ATT_HEADS):
        acc0, acc1 = acc_ref[2 * hh], acc_ref[2 * hh + 1]
        o = acc0[:, 0:DIFF_V_DIM] / acc0[:, DIFF_V_DIM:] - lam * (acc1[:, 0:DIFF_V_DIM] / acc1[:, DIFF_V_DIM:])
        ms = jnp.mean(o * o, axis=-1, keepdims=True)
        o = o * lax.rsqrt(ms + NORM_EPS) * sw_ref[...] * (1.0 - lambda_init)
        o_ref[0, :, hh * LANES:(hh + 1) * LANES] = o.astype(o_ref.dtype)


def _attn_prompt(q3, k3, v3, k_meta, v_meta, lam_vecs, subln_w, lambda_init, meta_pad):
    nb, seq_len, _ = q3.shape
    n_q = seq_len // ATT_TQ
    kern = functools.partial(_attn_prompt_kernel, lambda_init=lambda_init, meta_pad=meta_pad)
    width = ATT_HEADS * LANES
    n_streams = 2 * ATT_HEADS
    q_spec = pl.BlockSpec((1, ATT_TQ, width), lambda b, h, i: (b, i, h))
    kv_spec = pl.BlockSpec((1, seq_len, width), lambda b, h, i: (b, 0, h))
    meta_spec = pl.BlockSpec((CHUNK, width), lambda b, h, i: (0, h))
    return pl.pallas_call(
        kern,
        grid=(nb, DIFF_HEADS // ATT_HEADS, n_q),
        in_specs=[q_spec, kv_spec, kv_spec, meta_spec, meta_spec,
                  _const_spec(lam_vecs.shape), _const_spec(subln_w.shape)],
        out_specs=q_spec,
        out_shape=jax.ShapeDtypeStruct((nb, seq_len, DIFF_HEADS * DIFF_V_DIM), BF16),
        scratch_shapes=[pltpu.VMEM((n_streams, n_q, ATT_TQ, ATT_TQ), F32),
                        pltpu.VMEM((n_streams, ATT_TQ, CHUNK), F32),
                        pltpu.VMEM((n_streams, ATT_TQ, LANES), F32),
                        pltpu.VMEM((n_streams, ATT_TQ, DIFF_V_DIM + LANES), F32)],
        compiler_params=_cparams(("parallel", "parallel", "arbitrary")),
        name="attn_prompt",
    )(q3, k3, v3, k_meta, v_meta, lam_vecs, subln_w)


N_SUB = 2 * DIFF_HEADS


def _decode_attn_init(j, scratch):
    m_ref, l_ref, acc_ref = scratch

    @pl.when(j == 0)
    def _():
        m_ref[...] = jnp.full(m_ref.shape, -jnp.inf, F32)
        l_ref[...] = jnp.zeros(l_ref.shape, F32)
        acc_ref[...] = jnp.zeros(acc_ref.shape, F32)


def _decode_query_matrix(q_ref):
    row = lax.broadcasted_iota(jnp.int32, (N_SUB, HQK), 0)
    col = lax.broadcasted_iota(jnp.int32, (N_SUB, HQK), 1)
    q_mask = (col // DIFF_QK_DIM) == (2 * (row % DIFF_HEADS) + row // DIFF_HEADS)
    return jnp.where(q_mask, q_ref[0], 0.0)


def _decode_attn_step(npg, refs, scratch):
    k_refs = refs[0:npg]
    v_refs = refs[npg:2 * npg]
    q_ref, rep_ref = refs[2 * npg], refs[2 * npg + 3]
    m_ref, l_ref, acc_ref = scratch

    row = lax.broadcasted_iota(jnp.int32, (N_SUB, HQK), 0)
    col = lax.broadcasted_iota(jnp.int32, (N_SUB, HQK), 1)
    p_mask = (col % DIFF_HEADS) == (row % DIFF_HEADS)
    q_bdb = _decode_query_matrix(q_ref).astype(BF16)

    s = [_dot(q_bdb, k_refs[u][0].astype(BF16)) for u in range(npg)]
    m_old = m_ref[...]
    m_new = m_old
    for u in range(npg):
        m_new = jnp.maximum(m_new, jnp.max(s[u], axis=-1, keepdims=True))
    alpha = jnp.exp2(m_old - m_new)
    p = [jnp.exp2(s[u] - m_new) for u in range(npg)]
    l_new = alpha * l_ref[...]
    for u in range(npg):
        l_new = l_new + jnp.sum(p[u], axis=-1, keepdims=True)
    p_rep = _dot(jnp.concatenate(p, axis=0).astype(BF16), rep_ref[...])
    acc = alpha * acc_ref[...]
    for u in range(npg):
        p_exp = jnp.where(p_mask, p_rep[u * N_SUB:(u + 1) * N_SUB], 0.0).astype(BF16)
        acc = acc + _dot(p_exp, v_refs[u][0].astype(BF16))
    m_ref[...] = m_new
    l_ref[...] = l_new
    acc_ref[...] = acc


def _decode_attn_final(j, n_steps, npg, refs, o_ref, scratch, lambda_init):
    q_ref, kn_ref, vn_ref, _, lam_ref, sw_ref = refs[2 * npg:]
    m_ref, l_ref, acc_ref = scratch

    @pl.when(j == n_steps - 1)
    def _():
        q_bd = _decode_query_matrix(q_ref)
        m_new, l_new, acc = m_ref[...], l_ref[...], acc_ref[...]
        s_new = jnp.sum(q_bd * kn_ref[0], axis=-1, keepdims=True)
        m_fin = jnp.maximum(m_new, s_new)
        a_fin = jnp.exp2(m_new - m_fin)
        p_new = jnp.exp2(s_new - m_fin)
        l_fin = a_fin * l_new + p_new
        v_new = jnp.concatenate([vn_ref[0], vn_ref[0]], axis=0)
        out = (a_fin * acc + p_new * v_new) / l_fin
        lam = _diff_lambda(lam_ref, lambda_init)
        o = out[0:DIFF_HEADS] - lam * out[DIFF_HEADS:N_SUB]
        ms = jnp.mean(o * o, axis=-1, keepdims=True)
        o_ref[0] = (o * lax.rsqrt(ms + NORM_EPS) * sw_ref[...] * (1.0 - lambda_init)).astype(o_ref.dtype)


def _fused_call(host_kernel, name, grid, semantics, args, in_specs, out_specs, out_shape, scratch_shapes=(),
                dec=None):
    if dec is None:
        return pl.pallas_call(host_kernel, grid=grid, in_specs=in_specs, out_specs=out_specs, out_shape=out_shape,
                              scratch_shapes=list(scratch_shapes), compiler_params=_cparams(semantics),
                              name=name)(*args)
    page_table, k_pages, v_pages, q, k_new, v_new, lam_vecs, subln_w, lambda_init = dec
    n_seq, n_pages = page_table.shape
    n_steps = math.prod(grid)
    spq = n_steps // n_seq
    npg = n_pages // spq
    assert spq * n_seq == n_steps and npg * spq == n_pages and v_pages.shape[1:] == k_pages.shape[1:]
    page_blk = (1,) + k_pages.shape[1:]

    def step_of(ids):
        t = ids[0]
        for size, i in zip(grid[1:], ids[1:]):
            t = t * size + i
        return t

    groups = page_table.reshape(n_steps, npg)

    def page_spec(u):
        return pl.BlockSpec(page_blk, lambda *ids_pt: (ids_pt[-1][step_of(ids_pt[:-1]), u], 0, 0))

    def seq_of(t):
        if spq & (spq - 1) == 0:
            return t >> (spq.bit_length() - 1)
        return t // spq

    per_seq = lambda *ids_pt: (seq_of(step_of(ids_pt[:-1])), 0, 0)
    const2 = lambda *_: (0, 0)
    rep = (jnp.arange(PAGE_SIZE)[:, None] == jnp.arange(PAGE_SIZE * DIFF_HEADS)[None, :] // DIFF_HEADS).astype(BF16)
    dec_specs = [page_spec(u) for u in range(npg)] + [page_spec(u) for u in range(npg)] + [
        pl.BlockSpec((1, 1, HQK), per_seq), pl.BlockSpec((1, 1, HQK), per_seq),
        pl.BlockSpec((1, DIFF_HEADS, DIFF_V_DIM), per_seq),
        pl.BlockSpec(rep.shape, const2), pl.BlockSpec(lam_vecs.shape, const2), pl.BlockSpec(subln_w.shape, const2)]
    dec_args = [k_pages] * npg + [v_pages] * npg + [q, k_new, v_new, rep, lam_vecs, subln_w]
    n_in, n_out, n_scr, n_dec = len(in_specs), len(out_specs), len(scratch_shapes), len(dec_specs)

    def kern(pt_ref, *refs):
        ins, dec_in = refs[:n_in], refs[n_in:n_in + n_dec]
        outs = refs[n_in + n_dec:n_in + n_dec + n_out]
        o_dec = refs[n_in + n_dec + n_out]
        scr = refs[n_in + n_dec + n_out + 1:n_in + n_dec + n_out + 1 + n_scr]
        dec_scr = refs[n_in + n_dec + n_out + 1 + n_scr:]
        j = step_of([pl.program_id(a) for a in range(len(grid))]) % spq
        _decode_attn_init(j, dec_scr)
        host_kernel(*ins, *outs, *scr)
        _decode_attn_step(npg, dec_in, dec_scr)
        _decode_attn_final(j, spq, npg, dec_in, o_dec, dec_scr, lambda_init)

    grid_spec = pltpu.PrefetchScalarGridSpec(
        num_scalar_prefetch=1,
        grid=grid,
        in_specs=list(in_specs) + dec_specs,
        out_specs=list(out_specs) + [pl.BlockSpec((1, DIFF_HEADS, DIFF_V_DIM), per_seq)],
        scratch_shapes=list(scratch_shapes) + [pltpu.VMEM((N_SUB, 1), F32), pltpu.VMEM((N_SUB, 1), F32),
                                               pltpu.VMEM((N_SUB, DIFF_V_DIM), F32)],
    )
    return pl.pallas_call(
        kern,
        grid_spec=grid_spec,
        out_shape=list(out_shape) + [jax.ShapeDtypeStruct((n_seq, DIFF_HEADS, DIFF_V_DIM), BF16)],
        compiler_params=_cparams(("arbitrary",) * len(grid)),
        name=name + "_dec",
    )(groups, *args, *dec_args)


def _row_tiled_call(host_kernel, name, args, in_specs, out_specs, out_shape, n_steps):
    return _fused_call(host_kernel, name, (n_steps,), ("parallel",), args, in_specs, out_specs, out_shape)


def kernel(x_prompt, x_sample, cache_k, cache_v, state_ssm, state_conv, page_table, meta_tokens, a_w_in, a_conv_w, a_conv_b, a_dt_bias, a_A_log, a_D, a_norm_w, a_w_out, kv_w_k, kv_w_v, b_w_q, b_lambda, b_subln_w, b_w_o, ffn_w_gate, ffn_w_up, ffn_w_down, ln_mix_w, ln_mix_b, ln_ffn_w, ln_ffn_b):
    bp, seq, _ = x_prompt.shape
    bs = x_sample.shape[0]
    n_pages = page_table.shape[1]
    past_len = n_pages * PAGE_SIZE
    tm_p = ROW_TILE
    lambda_init = 0.8 - 0.6 * math.exp(-0.3 * 1)
    scale = DIFF_QK_DIM ** -0.5

    w_in = a_w_in[0]
    wz = w_in[:, :D_INNER].astype(BF16)
    wx = w_in[:, D_INNER:D_INNER + CONV_DIM].astype(BF16)
    wdt = jnp.pad(w_in[:, D_INNER + CONV_DIM:], ((0, 0), (0, LANES - SSM_HEADS))).astype(BF16)
    pad_h = lambda v: jnp.pad(v.reshape(1, SSM_HEADS), ((0, 0), (0, LANES - SSM_HEADS)))
    dt_bias, a_log = pad_h(a_dt_bias[0]), pad_h(a_A_log[0])
    conv_w, conv_b = a_conv_w[0], a_conv_b[0].reshape(1, CONV_DIM)
    d_exp = jnp.repeat(a_D[0], SSM_HEAD_DIM).reshape(1, D_INNER)
    norm_w = a_norm_w[0].reshape(1, D_INNER)
    w_out = a_w_out[0].astype(BF16)
    wk, wv = kv_w_k.astype(BF16), kv_w_v.astype(BF16)
    wq = (b_w_q[0] * (scale * LOG2_E)).astype(BF16)
    wo = b_w_o[0].astype(BF16)
    wg, wu, wd = ffn_w_gate.astype(BF16), ffn_w_up.astype(BF16), ffn_w_down.astype(BF16)
    ln = lambda t, l: t[l].reshape(1, D_MODEL)
    subln_w = b_subln_w[0].reshape(1, DIFF_V_DIM)
    lam_vecs = b_lambda[0]

    n_main = bp * seq
    n_aux = CHUNK + bs
    hm = x_prompt.reshape(n_main, D_MODEL)
    ha = jnp.concatenate([jnp.zeros((PAD_ROWS, D_MODEL), F32), meta_tokens, x_sample.reshape(bs, D_MODEL)], axis=0)
    smp = lambda t: t[CHUNK:]

    z_a, xbc_a, dtr_a = _in_proj(ha, wz, wx, wdt, n_aux)
    y_meta, tail_meta, _, st_meta = _mamba_prompt(
        xbc_a, z_a, dtr_a, jnp.zeros((SUBLANES, CONV_DIM), F32), jnp.zeros((SSM_STATE, D_INNER), F32),
        conv_w, conv_b, dt_bias, a_log, d_exp, norm_w, 1, 1, PAD_ROWS)
    prev = jnp.transpose(state_conv[0], (1, 0, 2))
    xc_s, conv_s, dt_s, decay_s = _conv_step(smp(xbc_a), prev, conv_w, conv_b, smp(dtr_a), dt_bias, a_log)
    per_ch = lambda t: jnp.repeat(t[:, :SSM_HEADS], SSM_HEAD_DIM, axis=1)
    y_s, ssm_s = _ssd_step(xc_s[:, :D_INNER], smp(z_a), per_ch(dt_s), per_ch(decay_s),
                           xc_s[:, D_INNER:D_INNER + GN].reshape(bs, SSM_GROUPS, SSM_STATE),
                           xc_s[:, D_INNER + GN:].reshape(bs, SSM_GROUPS, SSM_STATE),
                           state_ssm[0], d_exp, norm_w)
    y_a = jnp.concatenate([y_meta, y_s], axis=0)

    ha = _proj_ffn_ln(y_a, ha, w_out, ln(ln_mix_w, 0), ln(ln_mix_b, 0),
                      wg[0], wu[0], wd[0], ln(ln_ffn_w, 0), ln(ln_ffn_b, 0), n_aux)
    pos_a = jnp.concatenate([jnp.zeros((PAD_ROWS,), jnp.int32), jnp.arange(N_META),
                             jnp.full((bs,), past_len, jnp.int32)])
    k_a, v_a, kb_a, vb_a, q_a = _kvq(ha, wk, wv, wq, *_rope_tables(pos_a), n_aux, F32)
    hs, k_s, v_s, q_s = smp(ha), smp(k_a), smp(v_a), smp(q_a)

    n_pool = cache_k.shape[0]
    k_pages = jnp.transpose(cache_k, (0, 2, 3, 4, 1)).reshape(n_pool, HQK, PAGE_SIZE)
    v_pages = cache_v.reshape(n_pool, PAGE_SIZE * DIFF_HEADS, DIFF_V_DIM)
    dec = (page_table, k_pages, v_pages, q_s.reshape(bs, 1, HQK), k_s.reshape(bs, 1, HQK),
           v_s.reshape(bs, DIFF_HEADS, DIFF_V_DIM), lam_vecs, subln_w, lambda_init)
    z_m, xc_m, dtr_m, raw_tails = _in_proj_conv(hm, tail_meta[0], wz, wx, wdt, conv_w, conv_b, tm_p, seq)
    y_m, _, ssm_p, _, o_s = _mamba_prompt(xc_m, z_m, dtr_m, tail_meta[0], st_meta[0], conv_w, conv_b,
                                          dt_bias, a_log, d_exp, norm_w, bp, seq // CHUNK, 0, True, dec)
    conv_tail = raw_tails.reshape(bp, seq // tm_p, SUBLANES, CONV_DIM)[:, -1]
    hm = _proj_ffn_ln(y_m, hm, w_out, ln(ln_mix_w, 0), ln(ln_mix_b, 0),
                      wg[0], wu[0], wd[0], ln(ln_ffn_w, 0), ln(ln_ffn_b, 0), tm_p)
    k_m, v_m, kb_m, vb_m, qb_m = _kvq(hm, wk, wv, wq, *_rope_tables(N_META + jnp.arange(seq)), tm_p, BF16)

    to3 = lambda t: t.reshape(bp, seq, D_MODEL)
    o_m = _attn_prompt(to3(qb_m), to3(kb_m), to3(vb_m), kb_a, vb_a, lam_vecs, subln_w, lambda_init, PAD_ROWS)
    hm = _proj_ffn_ln(o_m.reshape(n_main, D_MODEL), hm, wo, ln(ln_mix_w, 1), ln(ln_mix_b, 1),
                      wg[1], wu[1], wd[1], ln(ln_ffn_w, 1), ln(ln_ffn_b, 1), tm_p)

    hs = _proj_ffn_ln(o_s.reshape(bs, D_MODEL), hs, wo, ln(ln_mix_w, 1), ln(ln_mix_b, 1),
                      wg[1], wu[1], wd[1], ln(ln_ffn_w, 1), ln(ln_ffn_b, 1), bs)

    with_meta = lambda meta, main: jnp.concatenate(
        [jnp.broadcast_to(meta[None, PAD_ROWS:CHUNK], (bp, N_META, D_MODEL)), to3(main)], axis=1)
    y_prompt = to3(hm)
    k_prompt = with_meta(k_a, k_m).reshape(bp, N_META + seq, DIFF_HEADS, 2, DIFF_QK_DIM)
    v_prompt = with_meta(v_a, v_m).reshape(bp, N_META + seq, DIFF_HEADS, DIFF_V_DIM)
    ssm_prompt = ssm_p[None]
    conv_prompt = conv_tail[None, :, SUBLANES - (D_CONV - 1):]
    y_sample = hs.reshape(bs, 1, D_MODEL)
    k_sample = k_s.reshape(bs, 1, DIFF_HEADS, 2, DIFF_QK_DIM)
    v_sample = v_s.reshape(bs, 1, DIFF_HEADS, DIFF_V_DIM)
    ssm_sample = ssm_s[None]
    conv_sample = jnp.transpose(conv_s, (1, 0, 2))[None]
    return (y_prompt, y_sample, k_prompt, v_prompt, ssm_prompt, conv_prompt, k_sample, v_sample, ssm_sample,
            conv_sample)
```

```python
import functools
import math

import jax
import jax.numpy as jnp
from jax import lax
from jax.experimental import pallas as pl
from jax.experimental.pallas import tpu as pltpu

F32 = jnp.float32
BF16 = jnp.bfloat16

D_MODEL = 1024
N_META = 16
D_INNER = 2048
SSM_HEADS = 32
SSM_HEAD_DIM = 64
SSM_GROUPS = 4
SSM_STATE = 128
D_CONV = 4
GN = SSM_GROUPS * SSM_STATE
CONV_DIM = D_INNER + 2 * GN
DIFF_HEADS = 8
DIFF_QK_DIM = 64
DIFF_V_DIM = 128
HQK = DIFF_HEADS * 2 * DIFF_QK_DIM
ROT_DIM = DIFF_QK_DIM // 4
ROPE_THETA = 500000.0
D_FF = 2816
DEPTH = 2
DEEPNORM_ALPHA = (2 * DEPTH) ** 0.25
NORM_EPS = 1e-5
PAGE_SIZE = 128
LOG2_E = math.log2(math.e)

LANES = 128
SUBLANES = 8
VMEM_LIMIT_BYTES = 56 * 1024 * 1024

CHUNK = 128
CONV_COLS = 512
PAD_ROWS = CHUNK - N_META
ATT_TQ = 512
ATT_HEADS = 2
DEC_PAGES_PER_STEP = 8
ROW_TILE = 512


def _cparams(sem):
    return pltpu.CompilerParams(dimension_semantics=sem, vmem_limit_bytes=VMEM_LIMIT_BYTES)


def _const_spec(shape):
    nd = len(shape)
    return pl.BlockSpec(shape, lambda *_: (0,) * nd)


def _weight_spec(shape):
    nd = len(shape)
    return pl.BlockSpec(shape, lambda *_: (0,) * nd, pipeline_mode=pl.Buffered(1))


def _ROW(i, *_):
    return (i, 0)


def _layer_norm(x, w, b):
    mu = jnp.mean(x, axis=-1, keepdims=True)
    xc = x - mu
    var = jnp.mean(xc * xc, axis=-1, keepdims=True)
    return xc * lax.rsqrt(var + NORM_EPS) * w + b


def _silu(x):
    h = 0.5 * x
    return h + h * jnp.tanh(h)


def _softplus(x):
    return jnp.maximum(x, 0.0) + jnp.log1p(jnp.exp(-jnp.abs(x)))


def _dot(a, b):
    return jnp.dot(a, b, preferred_element_type=F32)


def _dot_nt(a, b):
    return lax.dot_general(a, b, (((1,), (1,)), ((), ())), preferred_element_type=F32)


NCHUNK = 512


def _in_proj_kernel(x_ref, wz_ref, wx_ref, wdt_ref, z_ref, xbc_ref, dt_ref):
    xb = x_ref[...].astype(BF16)
    for n0 in range(0, D_INNER, NCHUNK):
        z_ref[:, n0:n0 + NCHUNK] = _dot(xb, wz_ref[:, n0:n0 + NCHUNK])
    for n0 in range(0, CONV_DIM, NCHUNK):
        xbc_ref[:, n0:n0 + NCHUNK] = _dot(xb, wx_ref[:, n0:n0 + NCHUNK])
    dt_ref[...] = _dot(xb, wdt_ref[...])


def _in_proj(x, wz, wx, wdt, tm):
    m = x.shape[0]
    return _row_tiled_call(
        _in_proj_kernel, "in_proj", (x, wz, wx, wdt),
        in_specs=[pl.BlockSpec((tm, D_MODEL), _ROW), _weight_spec(wz.shape), _weight_spec(wx.shape),
                  _weight_spec(wdt.shape)],
        out_specs=[pl.BlockSpec((tm, D_INNER), _ROW), pl.BlockSpec((tm, CONV_DIM), _ROW),
                   pl.BlockSpec((tm, LANES), _ROW)],
        out_shape=[jax.ShapeDtypeStruct((m, D_INNER), F32), jax.ShapeDtypeStruct((m, CONV_DIM), F32),
                   jax.ShapeDtypeStruct((m, LANES), F32)],
        n_steps=m // tm)


def _causal_conv_silu(ext, cw_ref, cb_ref, cols):
    prev1 = pltpu.roll(ext, 1, 0)
    u = cw_ref[1:2, cols] * ext + cw_ref[0:1, cols] * prev1
    conv = (cb_ref[:, cols] + cw_ref[3:4, cols] * ext[SUBLANES:] + cw_ref[2:3, cols] * prev1[SUBLANES:]
            + pltpu.roll(u, 2, 0)[SUBLANES:])
    return _silu(conv)


def _mamba_prompt_kernel(xbc_ref, z_ref, dtr_ref, tail0_ref, st0_ref, cw_ref, cb_ref, dtb_ref, alog_ref, dexp_ref,
                         nw_ref, y_ref, conv_out_ref, ssm_out_ref, st_out_ref,
                         ext_ref, st_ref, xc_ref, ybuf_ref, *, pad_rows):
    c = pl.program_id(1)
    n_chunks = pl.num_programs(1)

    @pl.when(c == 0)
    def _():
        ext_ref[0:SUBLANES, :] = tail0_ref[...]
        st_ref[...] = st0_ref[...]

    x_raw = xbc_ref[...]
    ext_ref[SUBLANES:SUBLANES + CHUNK, :] = x_raw
    for n0 in range(0, CONV_DIM, CONV_COLS):
        cols = slice(n0, n0 + CONV_COLS)
        xc_ref[:, cols] = _causal_conv_silu(ext_ref[:, cols], cw_ref, cb_ref, cols)
    ext_ref[0:SUBLANES, :] = x_raw[CHUNK - SUBLANES:CHUNK, :]

    row = lax.broadcasted_iota(jnp.int32, (CHUNK, LANES), 0)
    col = lax.broadcasted_iota(jnp.int32, (CHUNK, LANES), 1)
    dt = _softplus(dtr_ref[...] + dtb_ref[...])
    if pad_rows:
        dt = jnp.where(jnp.logical_or(c > 0, row >= pad_rows), dt, 0.0)
    a = dt * (-jnp.exp(alog_ref[...]))
    tri = (row >= col).astype(F32)
    cs = jnp.dot(tri, a, preferred_element_type=F32, precision=lax.Precision.HIGHEST)
    cs_t = cs.T
    dt_t = dt.T
    e_cs = jnp.exp(cs)
    causal = row >= col
    lane_lo = col < SSM_HEAD_DIM

    for g in range(SSM_GROUPS):
        b_g = xc_ref[:, D_INNER + g * SSM_STATE:D_INNER + (g + 1) * SSM_STATE]
        c_g = xc_ref[:, D_INNER + GN + g * SSM_STATE:D_INNER + GN + (g + 1) * SSM_STATE]
        c_gb = c_g.astype(BF16)
        b_gb = b_g.astype(BF16)
        cb = _dot_nt(c_gb, b_gb)
        bt = b_g.T
        g0 = g * (D_INNER // SSM_GROUPS)
        y_off_g = _dot(c_gb, st_ref[:, g0:g0 + D_INNER // SSM_GROUPS].astype(BF16))
        for jp in range(D_INNER // SSM_GROUPS // LANES):
            lo = g0 + jp * LANES
            h0 = lo // SSM_HEAD_DIM
            m_parts, mp_parts, dec_parts, ecol_parts = [], [], [], []
            for h in (h0, h0 + 1):
                colv = cs[:, h:h + 1]
                rowv = cs_t[h:h + 1, :]
                dtrow = dt_t[h:h + 1, :]
                last = cs[CHUNK - 1:CHUNK, h:h + 1]
                lmat = jnp.exp(jnp.where(causal, colv - rowv, -jnp.inf))
                m_parts.append((cb * lmat * dtrow).astype(BF16))
                mp_parts.append((bt * (jnp.exp(last - rowv) * dtrow)).astype(BF16))
                dec_parts.append(jnp.exp(last))
                ecol_parts.append(e_cs[:, h:h + 1])
            lhs = jnp.concatenate([jnp.concatenate(m_parts, axis=1),
                                   jnp.concatenate(mp_parts, axis=1)], axis=0)
            x_pair = xc_ref[:, lo:lo + LANES]
            rhs = jnp.concatenate([jnp.where(lane_lo, x_pair, 0.0).astype(BF16),
                                   jnp.where(lane_lo, 0.0, x_pair).astype(BF16)], axis=0)
            res = _dot(lhs, rhs)
            y_diag = res[0:CHUNK]
            d_state = res[CHUNK:2 * CHUNK]
            decay = jnp.where(lane_lo, dec_parts[0], dec_parts[1])
            ecol = jnp.where(lane_lo, ecol_parts[0], ecol_parts[1])
            st_ref[:, lo:lo + LANES] = st_ref[:, lo:lo + LANES] * decay + d_state
            y_pair = y_diag + y_off_g[:, jp * LANES:(jp + 1) * LANES] * ecol + x_pair * dexp_ref[:, lo:lo + LANES]
            ybuf_ref[:, lo:lo + LANES] = y_pair

    gated = ybuf_ref[...] * _silu(z_ref[...])
    ms = jnp.mean(gated * gated, axis=-1, keepdims=True)
    y_ref[...] = (gated * lax.rsqrt(ms + NORM_EPS) * nw_ref[...]).astype(y_ref.dtype)

    @pl.when(c == n_chunks - 1)
    def _():
        conv_out_ref[0] = xbc_ref[CHUNK - SUBLANES:CHUNK, :]
        st_out_ref[0] = st_ref[...]
        for jp in range(D_INNER // LANES):
            t = st_ref[:, jp * LANES:(jp + 1) * LANES].T
            ssm_out_ref[0, 2 * jp] = t[0:SSM_HEAD_DIM]
            ssm_out_ref[0, 2 * jp + 1] = t[SSM_HEAD_DIM:2 * SSM_HEAD_DIM]


def _mamba_prompt(xbc, z, dt_raw, tail0, st0, conv_w, conv_b, dt_bias, a_log, d_exp, norm_w, nb, n_chunks, pad_rows,
                  dec=None):
    m = nb * n_chunks * CHUNK
    blk = lambda b, c, *_: (b * n_chunks + c, 0)
    per_seq3 = lambda b, c, *_: (b, 0, 0)
    return _fused_call(
        functools.partial(_mamba_prompt_kernel, pad_rows=pad_rows), "mamba_prompt", (nb, n_chunks),
        ("parallel", "arbitrary"), (xbc, z, dt_raw, tail0, st0, conv_w, conv_b, dt_bias, a_log, d_exp, norm_w),
        in_specs=[pl.BlockSpec((CHUNK, CONV_DIM), blk), pl.BlockSpec((CHUNK, D_INNER), blk),
                  pl.BlockSpec((CHUNK, LANES), blk), _const_spec(tail0.shape), _const_spec(st0.shape),
                  _const_spec(conv_w.shape), _const_spec(conv_b.shape), _const_spec(dt_bias.shape),
                  _const_spec(a_log.shape), _const_spec(d_exp.shape), _const_spec(norm_w.shape)],
        out_specs=[pl.BlockSpec((CHUNK, D_INNER), blk),
                   pl.BlockSpec((1, SUBLANES, CONV_DIM), per_seq3),
                   pl.BlockSpec((1, SSM_HEADS, SSM_HEAD_DIM, SSM_STATE), lambda b, c, *_: (b, 0, 0, 0)),
                   pl.BlockSpec((1, SSM_STATE, D_INNER), per_seq3)],
        out_shape=[jax.ShapeDtypeStruct((m, D_INNER), BF16),
                   jax.ShapeDtypeStruct((nb, SUBLANES, CONV_DIM), F32),
                   jax.ShapeDtypeStruct((nb, SSM_HEADS, SSM_HEAD_DIM, SSM_STATE), F32),
                   jax.ShapeDtypeStruct((nb, SSM_STATE, D_INNER), F32)],
        scratch_shapes=[pltpu.VMEM((SUBLANES + CHUNK, CONV_DIM), F32),
                        pltpu.VMEM((SSM_STATE, D_INNER), F32),
                        pltpu.VMEM((CHUNK, CONV_DIM), F32),
                        pltpu.VMEM((CHUNK, D_INNER), F32)],
        dec=dec)


def _conv_step_kernel(xbc_ref, prev_ref, cw_ref, cb_ref, dtr_ref, dtb_ref, alog_ref,
                      xc_ref, newconv_ref, dt_ref, decay_ref):
    x_raw = xbc_ref[...]
    conv = cb_ref[...] + cw_ref[3:4, :] * x_raw
    for k in range(D_CONV - 1):
        conv = conv + cw_ref[k:k + 1, :] * prev_ref[k]
    xc_ref[...] = _silu(conv)
    newconv_ref[0] = prev_ref[1]
    newconv_ref[1] = prev_ref[2]
    newconv_ref[2] = x_raw
    dt = _softplus(dtr_ref[...] + dtb_ref[...])
    dt_ref[...] = dt
    decay_ref[...] = jnp.exp(dt * (-jnp.exp(alog_ref[...])))


def _conv_step(xbc, prev, conv_w, conv_b, dt_raw, dt_bias, a_log):
    nb = xbc.shape[0]
    return pl.pallas_call(
        _conv_step_kernel,
        out_shape=[jax.ShapeDtypeStruct((nb, CONV_DIM), F32), jax.ShapeDtypeStruct((D_CONV - 1, nb, CONV_DIM), F32),
                   jax.ShapeDtypeStruct((nb, LANES), F32), jax.ShapeDtypeStruct((nb, LANES), F32)],
        compiler_params=pltpu.CompilerParams(vmem_limit_bytes=VMEM_LIMIT_BYTES),
        name="conv_step",
    )(xbc, prev, conv_w, conv_b, dt_raw, dt_bias, a_log)


N_PAIRS = D_INNER // LANES
SSD_SEQS_PER_STEP = 2


def _ssd_step_kernel(x_ref, z_ref, dt_ref, dec_ref, b_ref, c_ref, h_ref, dsk_ref, nw_ref, y_ref, hout_ref):
    pad = jnp.zeros((LANES - N_PAIRS, LANES), F32)
    cols = lambda v: jnp.concatenate([v, pad], axis=0).T
    lane = lax.broadcasted_iota(jnp.int32, (LANES, LANES), 1)
    for s in range(x_ref.shape[0]):
        x = x_ref[s]
        x_t, dtx_t, dec_t = cols(x), cols(dt_ref[s] * x), cols(dec_ref[s])
        y_t = jnp.zeros((LANES, LANES), F32)
        for r in range(N_PAIRS):
            g = r // (N_PAIRS // SSM_GROUPS)
            b_row = b_ref[s, g:g + 1, :]
            c_row = c_ref[s, g:g + 1, :]
            dtx_col, dec_col = dtx_t[:, r:r + 1], dec_t[:, r:r + 1]
            h0 = h_ref[s, r]
            cb = jnp.sum(c_row * b_row, axis=-1, keepdims=True)
            y_col = dec_col * jnp.sum(h0 * c_row, axis=-1, keepdims=True) + cb * dtx_col
            hout_ref[s, r] = dec_col * h0 + dtx_col * b_row
            y_t = jnp.where(lane == r, y_col, y_t)
        y = y_t.T[0:N_PAIRS] + x * dsk_ref[...]
        gated = y * _silu(z_ref[s])
        ms = jnp.sum(jnp.sum(gated * gated, axis=-1, keepdims=True), axis=0, keepdims=True) / D_INNER
        y_ref[s] = (gated * lax.rsqrt(ms + NORM_EPS) * nw_ref[...]).astype(y_ref.dtype)


def _ssd_step(x, z, dt_ch, decay_ch, b3, c3, h0, d_skip, norm_w):
    nb = x.shape[0]
    ns = SSD_SEQS_PER_STEP if nb % SSD_SEQS_PER_STEP == 0 else 1
    per_b = lambda b: (b, 0, 0)
    vec = pl.BlockSpec((ns, N_PAIRS, LANES), per_b)
    pairs = lambda t: t.reshape(-1, N_PAIRS, LANES)
    st_spec = pl.BlockSpec((ns, N_PAIRS, LANES, SSM_STATE), lambda b: (b, 0, 0, 0))
    grp = pl.BlockSpec((ns, SSM_GROUPS, SSM_STATE), per_b)
    y, h_new = pl.pallas_call(
        _ssd_step_kernel,
        grid=(nb // ns,),
        in_specs=[vec, vec, vec, vec, grp, grp, st_spec, _const_spec((N_PAIRS, LANES)), _const_spec((N_PAIRS, LANES))],
        out_specs=[vec, st_spec],
        out_shape=[jax.ShapeDtypeStruct((nb, N_PAIRS, LANES), BF16),
                   jax.ShapeDtypeStruct((nb, N_PAIRS, LANES, SSM_STATE), F32)],
        compiler_params=_cparams(("parallel",)),
        name="ssd_step",
    )(pairs(x), pairs(z), pairs(dt_ch), pairs(decay_ch), b3, c3, h0.reshape(nb, N_PAIRS, LANES, SSM_STATE),
      d_skip.reshape(N_PAIRS, LANES), norm_w.reshape(N_PAIRS, LANES))
    return y.reshape(nb, D_INNER), h_new.reshape(h0.shape)


FF_CHUNK = 256


def _proj_ffn_ln_kernel(a_ref, r_ref, w_ref, mw_ref, mb_ref, wg_ref, wu_ref, wd_ref, fw_ref, fb_ref, o_ref):
    mix = _dot(a_ref[...], w_ref[...])
    x = _layer_norm(DEEPNORM_ALPHA * r_ref[...] + mix, mw_ref[...], mb_ref[...])
    xb = x.astype(BF16)
    acc = jnp.zeros(x.shape, F32)
    for f0 in range(0, D_FF, FF_CHUNK):
        gate = _dot(xb, wg_ref[:, f0:f0 + FF_CHUNK])
        up = _dot(xb, wu_ref[:, f0:f0 + FF_CHUNK])
        act = (_silu(gate) * up).astype(BF16)
        acc = acc + _dot(act, wd_ref[f0:f0 + FF_CHUNK, :])
    o_ref[...] = _layer_norm(DEEPNORM_ALPHA * x + acc, fw_ref[...], fb_ref[...])


def _proj_ffn_ln(a, resid, w, mix_w, mix_b, wg, wu, wd, ffn_w, ffn_b, tm):
    m, k = a.shape
    vec = _const_spec((1, D_MODEL))
    return _row_tiled_call(
        _proj_ffn_ln_kernel, "proj_ffn_ln", (a, resid, w, mix_w, mix_b, wg, wu, wd, ffn_w, ffn_b),
        in_specs=[pl.BlockSpec((tm, k), _ROW), pl.BlockSpec((tm, D_MODEL), _ROW), _weight_spec(w.shape), vec, vec,
                  _weight_spec(wg.shape), _weight_spec(wu.shape), _weight_spec(wd.shape), vec, vec],
        out_specs=[pl.BlockSpec((tm, D_MODEL), _ROW)],
        out_shape=[jax.ShapeDtypeStruct((m, D_MODEL), F32)],
        n_steps=m // tm)[0]


def _rotary(x, cos_t, sin_a, sin_b):
    half = ROT_DIM // 2
    parts = []
    for j in range(x.shape[1] // LANES):
        xb = x[:, j * LANES:(j + 1) * LANES]
        fwd = pltpu.roll(xb, LANES - half, 1)
        bwd = pltpu.roll(xb, half, 1)
        parts.append(xb * cos_t + fwd * sin_a + bwd * sin_b)
    return jnp.concatenate(parts, axis=1)


def _kvq_kernel(h_ref, wk_ref, wv_ref, wq_ref, cos_ref, sa_ref, sb_ref, k_ref, v_ref, kb_ref, vb_ref, qb_ref):
    hb = h_ref[...].astype(BF16)
    cos_t, sin_a, sin_b = cos_ref[...], sa_ref[...], sb_ref[...]
    k = _rotary(_dot(hb, wk_ref[...]), cos_t, sin_a, sin_b)
    k_ref[...] = k
    kb_ref[...] = k.astype(BF16)
    v = _dot(hb, wv_ref[...])
    v_ref[...] = v
    vb_ref[...] = v.astype(BF16)
    q = _rotary(_dot(hb, wq_ref[...]), cos_t, sin_a, sin_b)
    qb_ref[...] = q.astype(qb_ref.dtype)


def _kvq(h, wk, wv, wq, cos_t, sin_a, sin_b, tm, q_dtype):
    m = h.shape[0]
    tab_blocks = cos_t.shape[0] // tm
    full = pl.BlockSpec((tm, D_MODEL), _ROW)
    tab = pl.BlockSpec((tm, LANES), lambda i, *_: (i % tab_blocks, 0))
    return _row_tiled_call(
        _kvq_kernel, "kvq_proj", (h, wk, wv, wq, cos_t, sin_a, sin_b),
        in_specs=[full, _weight_spec(wk.shape), _weight_spec(wv.shape), _weight_spec(wq.shape), tab, tab, tab],
        out_specs=[full, full, full, full, full],
        out_shape=[jax.ShapeDtypeStruct((m, D_MODEL), F32), jax.ShapeDtypeStruct((m, D_MODEL), F32),
                   jax.ShapeDtypeStruct((m, D_MODEL), BF16), jax.ShapeDtypeStruct((m, D_MODEL), BF16),
                   jax.ShapeDtypeStruct((m, D_MODEL), q_dtype)],
        n_steps=m // tm)


def _kvq_prompt_kernel(h_ref, wk_ref, wv_ref, wq_ref, cos_ref, sa_ref, sb_ref, kmeta_ref, vmeta_ref,
                       kb_ref, vb_ref, qb_ref, kfull_ref, vfull_ref, kbuf, vbuf, tile_sem, meta_sem,
                       *, tiles_per_seq):
    i = pl.program_id(0)
    tm = h_ref.shape[0]
    slot = i % 2
    seq_i = i // tiles_per_seq
    row0 = N_META + (i % tiles_per_seq) * tm

    def tile_copy(buf, full, s, b, r, which):
        return pltpu.make_async_copy(buf.at[s], full.at[b, pl.ds(r, tm), :], tile_sem.at[s, which])

    @pl.when(i >= 2)
    def _():
        tile_copy(kbuf, kfull_ref, slot, seq_i, row0, 0).wait()
        tile_copy(vbuf, vfull_ref, slot, seq_i, row0, 1).wait()

    hb = h_ref[...].astype(BF16)
    cos_t, sin_a, sin_b = cos_ref[...], sa_ref[...], sb_ref[...]
    k = _rotary(_dot(hb, wk_ref[...]), cos_t, sin_a, sin_b)
    kbuf[slot] = k
    tile_copy(kbuf, kfull_ref, slot, seq_i, row0, 0).start()
    kb_ref[...] = k.astype(BF16)
    v = _dot(hb, wv_ref[...])
    vbuf[slot] = v
    tile_copy(vbuf, vfull_ref, slot, seq_i, row0, 1).start()
    vb_ref[...] = v.astype(BF16)
    q = _rotary(_dot(hb, wq_ref[...]), cos_t, sin_a, sin_b)
    qb_ref[...] = q.astype(qb_ref.dtype)

    @pl.when(i % tiles_per_seq == 0)
    def _():
        copies = [pltpu.make_async_copy(src, full.at[seq_i, pl.ds(0, N_META), :], meta_sem.at[w])
                  for w, (src, full) in enumerate(((kmeta_ref, kfull_ref), (vmeta_ref, vfull_ref)))]
        for cp in copies:
            cp.start()
        for cp in copies:
            cp.wait()

    @pl.when(i == pl.num_programs(0) - 1)
    def _():
        for s in range(2):
            tile_copy(kbuf, kfull_ref, s, seq_i, row0, 0).wait()
            tile_copy(vbuf, vfull_ref, s, seq_i, row0, 1).wait()


def _kvq_prompt(h, wk, wv, wq, cos_t, sin_a, sin_b, k_aux, v_aux, tm, nb, seq):
    m = h.shape[0]
    n_steps = m // tm
    assert n_steps >= 2 and seq % tm == 0 and PAD_ROWS % N_META == 0
    full = pl.BlockSpec((tm, D_MODEL), _ROW)
    tab = pl.BlockSpec((tm, LANES), lambda i, *_: (i % (seq // tm), 0))
    meta = pl.BlockSpec((N_META, D_MODEL), lambda i, *_: (PAD_ROWS // N_META, 0))
    hbm = pl.BlockSpec(memory_space=pl.ANY)
    seq_out = jax.ShapeDtypeStruct((nb, N_META + seq, D_MODEL), F32)
    return _fused_call(
        functools.partial(_kvq_prompt_kernel, tiles_per_seq=seq // tm), "kvq_prompt", (n_steps,), ("arbitrary",),
        (h, wk, wv, wq, cos_t, sin_a, sin_b, k_aux, v_aux),
        in_specs=[full, _weight_spec(wk.shape), _weight_spec(wv.shape), _weight_spec(wq.shape), tab, tab, tab,
                  meta, meta],
        out_specs=[full, full, full, hbm, hbm],
        out_shape=[jax.ShapeDtypeStruct((m, D_MODEL), BF16)] * 3 + [seq_out, seq_out],
        scratch_shapes=[pltpu.VMEM((2, tm, D_MODEL), F32), pltpu.VMEM((2, tm, D_MODEL), F32),
                        pltpu.SemaphoreType.DMA((2, 2)), pltpu.SemaphoreType.DMA((2,))])


def _rope_tables(pos):
    half = ROT_DIM // 2
    inv_freq = ROPE_THETA ** (-jnp.arange(half, dtype=F32) * 2.0 / ROT_DIM)
    ang = pos.astype(F32)[:, None] * inv_freq[None, :]
    cos, sin = jnp.cos(ang), jnp.sin(ang)
    n = pos.shape[0]
    rest = DIFF_QK_DIM - ROT_DIM
    cos_sub = jnp.concatenate([cos, cos, jnp.ones((n, rest), F32)], axis=1)
    sa_sub = jnp.concatenate([-sin, jnp.zeros((n, half + rest), F32)], axis=1)
    sb_sub = jnp.concatenate([jnp.zeros((n, half), F32), sin, jnp.zeros((n, rest), F32)], axis=1)
    rep = lambda t: jnp.concatenate([t, t], axis=1)
    return rep(cos_sub), rep(sa_sub), rep(sb_sub)


def _diff_lambda(lam_ref, lambda_init):
    lv = lam_ref[...]
    s1 = jnp.sum(lv[0:1] * lv[1:2], axis=-1, keepdims=True)
    s2 = jnp.sum(lv[2:3] * lv[3:4], axis=-1, keepdims=True)
    return jnp.exp(s1) - jnp.exp(s2) + lambda_init


def _fold_lanes(x, op):
    r = x[:, 0:LANES]
    for t in range(1, x.shape[1] // LANES):
        r = op(r, x[:, t * LANES:(t + 1) * LANES])
    return r


def _exp2_minus(s, m_rep):
    return jnp.concatenate([jnp.exp2(s[:, t * LANES:(t + 1) * LANES] - m_rep)
                            for t in range(s.shape[1] // LANES)], axis=1)


def _attn_prompt_kernel(q_ref, k_ref, v_ref, km_ref, vm_ref, lam_ref, sw_ref, o_ref,
                        s_ref, sm_ref, mx_ref, acc_ref, *, lambda_init, meta_pad):
    i = pl.program_id(2)
    tq = ATT_TQ
    lane = lax.broadcasted_iota(jnp.int32, (tq, LANES), 1)
    heads = [hh * LANES for hh in range(ATT_HEADS) for _ in range(2)]
    q_sub = []
    for hh in range(ATT_HEADS):
        q = q_ref[0, :, hh * LANES:(hh + 1) * LANES].astype(F32)
        q_sub.append(jnp.where(lane < DIFF_QK_DIM, q, 0.0).astype(BF16))
        q_sub.append(jnp.where(lane < DIFF_QK_DIM, 0.0, q).astype(BF16))
    n_streams = len(q_sub)

    meta_keep = lax.broadcasted_iota(jnp.int32, (tq, CHUNK), 1) >= meta_pad
    for n in range(n_streams):
        s = jnp.where(meta_keep, _dot_nt(q_sub[n], km_ref[:, heads[n]:heads[n] + LANES]), -jnp.inf)
        sm_ref[n] = s
        mx_ref[n] = s

    def qk_block(j, keep):
        rows = pl.ds(pl.multiple_of(j * tq, tq), tq)
        for n in range(n_streams):
            s = _dot_nt(q_sub[n], k_ref[0, rows, heads[n]:heads[n] + LANES])
            if keep is not None:
                s = jnp.where(keep, s, -jnp.inf)
            s_ref[n, j] = s
            mx_ref[n] = jnp.maximum(mx_ref[n], _fold_lanes(s, jnp.maximum))

    def qk_body(j, carry):
        qk_block(j, None)
        return carry

    lax.fori_loop(0, i, qk_body, 0)
    qk_block(i, lax.broadcasted_iota(jnp.int32, (tq, tq), 1) <= lax.broadcasted_iota(jnp.int32, (tq, tq), 0))

    for n in range(n_streams):
        mx_ref[n] = jnp.broadcast_to(jnp.max(mx_ref[n], axis=-1, keepdims=True), (tq, LANES))

    ones_meta = jnp.ones((CHUNK, LANES), BF16)
    for n in range(n_streams):
        p = jnp.exp2(sm_ref[n] - mx_ref[n])
        v_meta = jnp.concatenate([vm_ref[:, heads[n]:heads[n] + LANES], ones_meta], axis=1)
        acc_ref[n] = _dot(p.astype(BF16), v_meta)

    ones_blk = jnp.ones((tq, LANES), BF16)

    def pv_body(j, carry):
        rows = pl.ds(pl.multiple_of(j * tq, tq), tq)
        for n in range(n_streams):
            vb = jnp.concatenate([v_ref[0, rows, heads[n]:heads[n] + LANES], ones_blk], axis=1)
            p = _exp2_minus(s_ref[n, j], mx_ref[n])
            acc_ref[n] = acc_ref[n] + _dot(p.astype(BF16), vb)
        return carry

    lax.fori_loop(0, i + 1, pv_body, 0)

    lam = _diff_lambda(lam_ref, lambda_init)
    for hh in range(ATT_HEADS):
        acc0, acc1 = acc_ref[2 * hh], acc_ref[2 * hh + 1]
        o = acc0[:, 0:DIFF_V_DIM] / acc0[:, DIFF_V_DIM:] - lam * (acc1[:, 0:DIFF_V_DIM] / acc1[:, DIFF_V_DIM:])
        ms = jnp.mean(o * o, axis=-1, keepdims=True)
        o = o * lax.rsqrt(ms + NORM_EPS) * sw_ref[...] * (1.0 - lambda_init)
        o_ref[0, :, hh * LANES:(hh + 1) * LANES] = o.astype(o_ref.dtype)


def _attn_prompt(q3, k3, v3, k_meta, v_meta, lam_vecs, subln_w, lambda_init, meta_pad):
    nb, seq_len, _ = q3.shape
    n_q = seq_len // ATT_TQ
    kern = functools.partial(_attn_prompt_kernel, lambda_init=lambda_init, meta_pad=meta_pad)
    width = ATT_HEADS * LANES
    n_streams = 2 * ATT_HEADS
    q_spec = pl.BlockSpec((1, ATT_TQ, width), lambda b, h, i: (b, i, h))
    kv_spec = pl.BlockSpec((1, seq_len, width), lambda b, h, i: (b, 0, h))
    meta_spec = pl.BlockSpec((CHUNK, width), lambda b, h, i: (0, h))
    return pl.pallas_call(
        kern,
        grid=(nb, DIFF_HEADS // ATT_HEADS, n_q),
        in_specs=[q_spec, kv_spec, kv_spec, meta_spec, meta_spec,
                  _const_spec(lam_vecs.shape), _const_spec(subln_w.shape)],
        out_specs=q_spec,
        out_shape=jax.ShapeDtypeStruct((nb, seq_len, DIFF_HEADS * DIFF_V_DIM), BF16),
        scratch_shapes=[pltpu.VMEM((n_streams, n_q, ATT_TQ, ATT_TQ), F32),
                        pltpu.VMEM((n_streams, ATT_TQ, CHUNK), F32),
                        pltpu.VMEM((n_streams, ATT_TQ, LANES), F32),
                        pltpu.VMEM((n_streams, ATT_TQ, DIFF_V_DIM + LANES), F32)],
        compiler_params=_cparams(("parallel", "parallel", "arbitrary")),
        name="attn_prompt",
    )(q3, k3, v3, k_meta, v_meta, lam_vecs, subln_w)


N_SUB = 2 * DIFF_HEADS


def _decode_attn_init(j, scratch):
    m_ref, l_ref, acc_ref = scratch

    @pl.when(j == 0)
    def _():
        m_ref[...] = jnp.full(m_ref.shape, -jnp.inf, F32)
        l_ref[...] = jnp.zeros(l_ref.shape, F32)
        acc_ref[...] = jnp.zeros(acc_ref.shape, F32)


def _decode_query_matrix(q_ref):
    row = lax.broadcasted_iota(jnp.int32, (N_SUB, HQK), 0)
    col = lax.broadcasted_iota(jnp.int32, (N_SUB, HQK), 1)
    q_mask = (col // DIFF_QK_DIM) == (2 * (row % DIFF_HEADS) + row // DIFF_HEADS)
    return jnp.where(q_mask, q_ref[0], 0.0)


def _decode_attn_step(npg, refs, scratch):
    k_refs = refs[0:npg]
    v_refs = refs[npg:2 * npg]
    q_ref, rep_ref = refs[2 * npg], refs[2 * npg + 3]
    m_ref, l_ref, acc_ref = scratch

    row = lax.broadcasted_iota(jnp.int32, (N_SUB, HQK), 0)
    col = lax.broadcasted_iota(jnp.int32, (N_SUB, HQK), 1)
    p_mask = (col % DIFF_HEADS) == (row % DIFF_HEADS)
    q_bdb = _decode_query_matrix(q_ref).astype(BF16)

    s = [_dot(q_bdb, k_refs[u][0].astype(BF16)) for u in range(npg)]
    m_old = m_ref[...]
    m_new = m_old
    for u in range(npg):
        m_new = jnp.maximum(m_new, jnp.max(s[u], axis=-1, keepdims=True))
    alpha = jnp.exp2(m_old - m_new)
    p = [jnp.exp2(s[u] - m_new) for u in range(npg)]
    l_new = alpha * l_ref[...]
    for u in range(npg):
        l_new = l_new + jnp.sum(p[u], axis=-1, keepdims=True)
    p_rep = _dot(jnp.concatenate(p, axis=0).astype(BF16), rep_ref[...])
    acc = alpha * acc_ref[...]
    for u in range(npg):
        p_exp = jnp.where(p_mask, p_rep[u * N_SUB:(u + 1) * N_SUB], 0.0).astype(BF16)
        acc = acc + _dot(p_exp, v_refs[u][0].astype(BF16))
    m_ref[...] = m_new
    l_ref[...] = l_new
    acc_ref[...] = acc


def _decode_attn_final(j, n_steps, npg, refs, o_ref, scratch, lambda_init):
    q_ref, kn_ref, vn_ref, _, lam_ref, sw_ref = refs[2 * npg:]
    m_ref, l_ref, acc_ref = scratch

    @pl.when(j == n_steps - 1)
    def _():
        q_bd = _decode_query_matrix(q_ref)
        m_new, l_new, acc = m_ref[...], l_ref[...], acc_ref[...]
        s_new = jnp.sum(q_bd * kn_ref[0], axis=-1, keepdims=True)
        m_fin = jnp.maximum(m_new, s_new)
        a_fin = jnp.exp2(m_new - m_fin)
        p_new = jnp.exp2(s_new - m_fin)
        l_fin = a_fin * l_new + p_new
        v_new = jnp.concatenate([vn_ref[0], vn_ref[0]], axis=0)
        out = (a_fin * acc + p_new * v_new) / l_fin
        lam = _diff_lambda(lam_ref, lambda_init)
        o = out[0:DIFF_HEADS] - lam * out[DIFF_HEADS:N_SUB]
        ms = jnp.mean(o * o, axis=-1, keepdims=True)
        o_ref[0] = (o * lax.rsqrt(ms + NORM_EPS) * sw_ref[...] * (1.0 - lambda_init)).astype(o_ref.dtype)


def _fused_call(host_kernel, name, grid, semantics, args, in_specs, out_specs, out_shape, scratch_shapes=(),
                dec=None):
    if dec is None:
        return pl.pallas_call(host_kernel, grid=grid, in_specs=in_specs, out_specs=out_specs, out_shape=out_shape,
                              scratch_shapes=list(scratch_shapes), compiler_params=_cparams(semantics),
                              name=name)(*args)
    page_table, k_pages, v_pages, q, k_new, v_new, lam_vecs, subln_w, lambda_init = dec
    n_seq, n_pages = page_table.shape
    n_steps = math.prod(grid)
    spq = n_steps // n_seq
    npg = n_pages // spq
    assert spq * n_seq == n_steps and npg * spq == n_pages and v_pages.shape[1:] == k_pages.shape[1:]
    page_blk = (1,) + k_pages.shape[1:]

    def step_of(ids):
        t = ids[0]
        for size, i in zip(grid[1:], ids[1:]):
            t = t * size + i
        return t

    groups = page_table.reshape(n_steps, npg)

    def page_spec(u):
        return pl.BlockSpec(page_blk, lambda *ids_pt: (ids_pt[-1][step_of(ids_pt[:-1]), u], 0, 0))

    def seq_of(t):
        if spq & (spq - 1) == 0:
            return t >> (spq.bit_length() - 1)
        return t // spq

    per_seq = lambda *ids_pt: (seq_of(step_of(ids_pt[:-1])), 0, 0)
    const2 = lambda *_: (0, 0)
    rep = (jnp.arange(PAGE_SIZE)[:, None] == jnp.arange(PAGE_SIZE * DIFF_HEADS)[None, :] // DIFF_HEADS).astype(BF16)
    dec_specs = [page_spec(u) for u in range(npg)] + [page_spec(u) for u in range(npg)] + [
        pl.BlockSpec((1, 1, HQK), per_seq), pl.BlockSpec((1, 1, HQK), per_seq),
        pl.BlockSpec((1, DIFF_HEADS, DIFF_V_DIM), per_seq),
        pl.BlockSpec(rep.shape, const2), pl.BlockSpec(lam_vecs.shape, const2), pl.BlockSpec(subln_w.shape, const2)]
    dec_args = [k_pages] * npg + [v_pages] * npg + [q, k_new, v_new, rep, lam_vecs, subln_w]
    n_in, n_out, n_scr, n_dec = len(in_specs), len(out_specs), len(scratch_shapes), len(dec_specs)

    def kern(pt_ref, *refs):
        ins, dec_in = refs[:n_in], refs[n_in:n_in + n_dec]
        outs = refs[n_in + n_dec:n_in + n_dec + n_out]
        o_dec = refs[n_in + n_dec + n_out]
        scr = refs[n_in + n_dec + n_out + 1:n_in + n_dec + n_out + 1 + n_scr]
        dec_scr = refs[n_in + n_dec + n_out + 1 + n_scr:]
        j = step_of([pl.program_id(a) for a in range(len(grid))]) % spq
        _decode_attn_init(j, dec_scr)
        host_kernel(*ins, *outs, *scr)
        _decode_attn_step(npg, dec_in, dec_scr)
        _decode_attn_final(j, spq, npg, dec_in, o_dec, dec_scr, lambda_init)

    grid_spec = pltpu.PrefetchScalarGridSpec(
        num_scalar_prefetch=1,
        grid=grid,
        in_specs=list(in_specs) + dec_specs,
        out_specs=list(out_specs) + [pl.BlockSpec((1, DIFF_HEADS, DIFF_V_DIM), per_seq)],
        scratch_shapes=list(scratch_shapes) + [pltpu.VMEM((N_SUB, 1), F32), pltpu.VMEM((N_SUB, 1), F32),
                                               pltpu.VMEM((N_SUB, DIFF_V_DIM), F32)],
    )
    return pl.pallas_call(
        kern,
        grid_spec=grid_spec,
        out_shape=list(out_shape) + [jax.ShapeDtypeStruct((n_seq, DIFF_HEADS, DIFF_V_DIM), BF16)],
        compiler_params=_cparams(("arbitrary",) * len(grid)),
        name=name + "_dec",
    )(groups, *args, *dec_args)


def _row_tiled_call(host_kernel, name, args, in_specs, out_specs, out_shape, n_steps):
    return _fused_call(host_kernel, name, (n_steps,), ("parallel",), args, in_specs, out_specs, out_shape)


def kernel(x_prompt, x_sample, cache_k, cache_v, state_ssm, state_conv, page_table, meta_tokens, a_w_in, a_conv_w, a_conv_b, a_dt_bias, a_A_log, a_D, a_norm_w, a_w_out, kv_w_k, kv_w_v, b_w_q, b_lambda, b_subln_w, b_w_o, ffn_w_gate, ffn_w_up, ffn_w_down, ln_mix_w, ln_mix_b, ln_ffn_w, ln_ffn_b):
    bp, seq, _ = x_prompt.shape
    bs = x_sample.shape[0]
    n_pages = page_table.shape[1]
    past_len = n_pages * PAGE_SIZE
    tm_p = ROW_TILE
    lambda_init = 0.8 - 0.6 * math.exp(-0.3 * 1)
    scale = DIFF_QK_DIM ** -0.5

    w_in = a_w_in[0]
    wz = w_in[:, :D_INNER].astype(BF16)
    wx = w_in[:, D_INNER:D_INNER + CONV_DIM].astype(BF16)
    wdt = jnp.pad(w_in[:, D_INNER + CONV_DIM:], ((0, 0), (0, LANES - SSM_HEADS))).astype(BF16)
    pad_h = lambda v: jnp.pad(v.reshape(1, SSM_HEADS), ((0, 0), (0, LANES - SSM_HEADS)))
    dt_bias, a_log = pad_h(a_dt_bias[0]), pad_h(a_A_log[0])
    conv_w, conv_b = a_conv_w[0], a_conv_b[0].reshape(1, CONV_DIM)
    d_exp = jnp.repeat(a_D[0], SSM_HEAD_DIM).reshape(1, D_INNER)
    norm_w = a_norm_w[0].reshape(1, D_INNER)
    w_out = a_w_out[0].astype(BF16)
    wk, wv = kv_w_k.astype(BF16), kv_w_v.astype(BF16)
    wq = (b_w_q[0] * (scale * LOG2_E)).astype(BF16)
    wo = b_w_o[0].astype(BF16)
    wg, wu, wd = ffn_w_gate.astype(BF16), ffn_w_up.astype(BF16), ffn_w_down.astype(BF16)
    ln = lambda t, l: t[l].reshape(1, D_MODEL)
    subln_w = b_subln_w[0].reshape(1, DIFF_V_DIM)
    lam_vecs = b_lambda[0]

    n_main = bp * seq
    n_aux = CHUNK + bs
    hm = x_prompt.reshape(n_main, D_MODEL)
    ha = jnp.concatenate([jnp.zeros((PAD_ROWS, D_MODEL), F32), meta_tokens, x_sample.reshape(bs, D_MODEL)], axis=0)
    smp = lambda t: t[CHUNK:]

    z_a, xbc_a, dtr_a = _in_proj(ha, wz, wx, wdt, n_aux)
    y_meta, tail_meta, _, st_meta = _mamba_prompt(
        xbc_a, z_a, dtr_a, jnp.zeros((SUBLANES, CONV_DIM), F32), jnp.zeros((SSM_STATE, D_INNER), F32),
        conv_w, conv_b, dt_bias, a_log, d_exp, norm_w, 1, 1, PAD_ROWS)
    prev = jnp.transpose(state_conv[0], (1, 0, 2))
    xc_s, conv_s, dt_s, decay_s = _conv_step(smp(xbc_a), prev, conv_w, conv_b, smp(dtr_a), dt_bias, a_log)
    per_ch = lambda t: jnp.repeat(t[:, :SSM_HEADS], SSM_HEAD_DIM, axis=1)
    y_s, ssm_s = _ssd_step(xc_s[:, :D_INNER], smp(z_a), per_ch(dt_s), per_ch(decay_s),
                           xc_s[:, D_INNER:D_INNER + GN].reshape(bs, SSM_GROUPS, SSM_STATE),
                           xc_s[:, D_INNER + GN:].reshape(bs, SSM_GROUPS, SSM_STATE),
                           state_ssm[0], d_exp, norm_w)
    y_a = jnp.concatenate([y_meta, y_s], axis=0)

    ha = _proj_ffn_ln(y_a, ha, w_out, ln(ln_mix_w, 0), ln(ln_mix_b, 0),
                      wg[0], wu[0], wd[0], ln(ln_ffn_w, 0), ln(ln_ffn_b, 0), n_aux)
    pos_a = jnp.concatenate([jnp.zeros((PAD_ROWS,), jnp.int32), jnp.arange(N_META),
                             jnp.full((bs,), past_len, jnp.int32)])
    k_a, v_a, kb_a, vb_a, q_a = _kvq(ha, wk, wv, wq, *_rope_tables(pos_a), n_aux, F32)
    hs, k_s, v_s, q_s = smp(ha), smp(k_a), smp(v_a), smp(q_a)

    n_pool = cache_k.shape[0]
    k_pages = jnp.transpose(cache_k, (0, 2, 3, 4, 1)).reshape(n_pool, HQK, PAGE_SIZE)
    v_pages = cache_v.reshape(n_pool, PAGE_SIZE * DIFF_HEADS, DIFF_V_DIM)
    dec = (page_table, k_pages, v_pages, q_s.reshape(bs, 1, HQK), k_s.reshape(bs, 1, HQK),
           v_s.reshape(bs, DIFF_HEADS, DIFF_V_DIM), lam_vecs, subln_w, lambda_init)
    z_m, xbc_m, dtr_m = _in_proj(hm, wz, wx, wdt, tm_p)
    y_m, conv_tail, ssm_p, _, o_s = _mamba_prompt(xbc_m, z_m, dtr_m, tail_meta[0], st_meta[0], conv_w, conv_b,
                                                  dt_bias, a_log, d_exp, norm_w, bp, seq // CHUNK, 0, dec)
    hm = _proj_ffn_ln(y_m, hm, w_out, ln(ln_mix_w, 0), ln(ln_mix_b, 0),
                      wg[0], wu[0], wd[0], ln(ln_ffn_w, 0), ln(ln_ffn_b, 0), tm_p)
    kb_m, vb_m, qb_m, k_full, v_full = _kvq_prompt(hm, wk, wv, wq, *_rope_tables(N_META + jnp.arange(seq)),
                                                   k_a, v_a, tm_p, bp, seq)

    to3 = lambda t: t.reshape(bp, seq, D_MODEL)
    o_m = _attn_prompt(to3(qb_m), to3(kb_m), to3(vb_m), kb_a, vb_a, lam_vecs, subln_w, lambda_init, PAD_ROWS)
    hm = _proj_ffn_ln(o_m.reshape(n_main, D_MODEL), hm, wo, ln(ln_mix_w, 1), ln(ln_mix_b, 1),
                      wg[1], wu[1], wd[1], ln(ln_ffn_w, 1), ln(ln_ffn_b, 1), tm_p)

    hs = _proj_ffn_ln(o_s.reshape(bs, D_MODEL), hs, wo, ln(ln_mix_w, 1), ln(ln_mix_b, 1),
                      wg[1], wu[1], wd[1], ln(ln_ffn_w, 1), ln(ln_ffn_b, 1), bs)

    y_prompt = to3(hm)
    k_prompt = k_full.reshape(bp, N_META + seq, DIFF_HEADS, 2, DIFF_QK_DIM)
    v_prompt = v_full.reshape(bp, N_META + seq, DIFF_HEADS, DIFF_V_DIM)
    ssm_prompt = ssm_p[None]
    conv_prompt = conv_tail[None, :, SUBLANES - (D_CONV - 1):]
    y_sample = hs.reshape(bs, 1, D_MODEL)
    k_sample = k_s.reshape(bs, 1, DIFF_HEADS, 2, DIFF_QK_DIM)
    v_sample = v_s.reshape(bs, 1, DIFF_HEADS, DIFF_V_DIM)
    ssm_sample = ssm_s[None]
    conv_sample = jnp.transpose(conv_s, (1, 0, 2))[None]
    return (y_prompt, y_sample, k_prompt, v_prompt, ssm_prompt, conv_prompt, k_sample, v_sample, ssm_sample,
            conv_sample)
```

```python
import functools
import math

import jax
import jax.numpy as jnp
from jax import lax
from jax.experimental import pallas as pl
from jax.experimental.pallas import tpu as pltpu

F32 = jnp.float32
BF16 = jnp.bfloat16

D_MODEL = 1024
N_META = 16
D_INNER = 2048
SSM_HEADS = 32
SSM_HEAD_DIM = 64
SSM_GROUPS = 4
SSM_STATE = 128
D_CONV = 4
GN = SSM_GROUPS * SSM_STATE
CONV_DIM = D_INNER + 2 * GN
DIFF_HEADS = 8
DIFF_QK_DIM = 64
DIFF_V_DIM = 128
HQK = DIFF_HEADS * 2 * DIFF_QK_DIM
ROT_DIM = DIFF_QK_DIM // 4
ROPE_THETA = 500000.0
D_FF = 2816
DEPTH = 2
DEEPNORM_ALPHA = (2 * DEPTH) ** 0.25
NORM_EPS = 1e-5
PAGE_SIZE = 128
LOG2_E = math.log2(math.e)

LANES = 128
SUBLANES = 8
VMEM_LIMIT_BYTES = 56 * 1024 * 1024

CHUNK = 128
CONV_COLS = 512
PAD_ROWS = CHUNK - N_META
ATT_TQ = 512
ATT_HEADS = 2
ROW_TILE = 512


def _cparams(sem):
    return pltpu.CompilerParams(dimension_semantics=sem, vmem_limit_bytes=VMEM_LIMIT_BYTES)


def _const_spec(shape):
    nd = len(shape)
    return pl.BlockSpec(shape, lambda *_: (0,) * nd)


def _weight_spec(shape):
    nd = len(shape)
    return pl.BlockSpec(shape, lambda *_: (0,) * nd, pipeline_mode=pl.Buffered(1))


def _ROW(i, *_):
    return (i, 0)


def _layer_norm(x, w, b):
    mu = jnp.mean(x, axis=-1, keepdims=True)
    xc = x - mu
    var = jnp.mean(xc * xc, axis=-1, keepdims=True)
    return xc * lax.rsqrt(var + NORM_EPS) * w + b


def _silu(x):
    h = 0.5 * x
    return h + h * jnp.tanh(h)


def _softplus(x):
    return jnp.maximum(x, 0.0) + jnp.log1p(jnp.exp(-jnp.abs(x)))


def _dot(a, b):
    return jnp.dot(a, b, preferred_element_type=F32)


def _dot_nt(a, b):
    return lax.dot_general(a, b, (((1,), (1,)), ((), ())), preferred_element_type=F32)


NCHUNK = 512


def _in_proj_kernel(x_ref, w_ref, wdt_ref, z_ref, xbc_ref, dt_ref):
    xb = x_ref[...].astype(BF16)
    for n0 in range(0, D_INNER, NCHUNK):
        z_ref[:, n0:n0 + NCHUNK] = _dot(xb, w_ref[:, n0:n0 + NCHUNK])
    for n0 in range(0, CONV_DIM, NCHUNK):
        xbc_ref[:, n0:n0 + NCHUNK] = _dot(xb, w_ref[:, D_INNER + n0:D_INNER + n0 + NCHUNK])
    dt_ref[...] = _dot(xb, wdt_ref[...])


def _in_proj(x, w, wdt, tm):
    m = x.shape[0]
    return _row_tiled_call(
        _in_proj_kernel, "in_proj", (x, w, wdt),
        in_specs=[pl.BlockSpec((tm, D_MODEL), _ROW), _weight_spec(w.shape), _weight_spec(wdt.shape)],
        out_specs=[pl.BlockSpec((tm, D_INNER), _ROW), pl.BlockSpec((tm, CONV_DIM), _ROW),
                   pl.BlockSpec((tm, LANES), _ROW)],
        out_shape=[jax.ShapeDtypeStruct((m, D_INNER), F32), jax.ShapeDtypeStruct((m, CONV_DIM), F32),
                   jax.ShapeDtypeStruct((m, LANES), F32)],
        n_steps=m // tm)


def _causal_conv_silu(ext, cw_ref, cb_ref, cols):
    prev1 = pltpu.roll(ext, 1, 0)
    u = cw_ref[1:2, cols] * ext + cw_ref[0:1, cols] * prev1
    conv = (cb_ref[:, cols] + cw_ref[3:4, cols] * ext[SUBLANES:] + cw_ref[2:3, cols] * prev1[SUBLANES:]
            + pltpu.roll(u, 2, 0)[SUBLANES:])
    return _silu(conv)


def _mamba_prompt_kernel(xbc_ref, z_ref, dtr_ref, tail0_ref, st0_ref, cw_ref, cb_ref, dtb_ref, alog_ref, dexp_ref,
                         nw_ref, y_ref, conv_out_ref, ssm_out_ref, st_out_ref,
                         ext_ref, st_ref, xc_ref, ybuf_ref, *, pad_rows):
    c = pl.program_id(1)
    n_chunks = pl.num_programs(1)

    @pl.when(c == 0)
    def _():
        ext_ref[0:SUBLANES, :] = tail0_ref[...]
        st_ref[...] = st0_ref[...]

    x_raw = xbc_ref[...]
    ext_ref[SUBLANES:SUBLANES + CHUNK, :] = x_raw
    for n0 in range(0, CONV_DIM, CONV_COLS):
        cols = slice(n0, n0 + CONV_COLS)
        xc_ref[:, cols] = _causal_conv_silu(ext_ref[:, cols], cw_ref, cb_ref, cols)
    ext_ref[0:SUBLANES, :] = x_raw[CHUNK - SUBLANES:CHUNK, :]

    row = lax.broadcasted_iota(jnp.int32, (CHUNK, LANES), 0)
    col = lax.broadcasted_iota(jnp.int32, (CHUNK, LANES), 1)
    dt = _softplus(dtr_ref[...] + dtb_ref[...])
    if pad_rows:
        dt = jnp.where(jnp.logical_or(c > 0, row >= pad_rows), dt, 0.0)
    a = dt * (-jnp.exp(alog_ref[...]))
    tri = (row >= col).astype(F32)
    cs = jnp.dot(tri, a, preferred_element_type=F32, precision=lax.Precision.HIGHEST)
    cs_t = cs.T
    dt_t = dt.T
    e_cs = jnp.exp(cs)
    causal = row >= col
    lane_lo = col < SSM_HEAD_DIM

    for g in range(SSM_GROUPS):
        b_g = xc_ref[:, D_INNER + g * SSM_STATE:D_INNER + (g + 1) * SSM_STATE]
        c_g = xc_ref[:, D_INNER + GN + g * SSM_STATE:D_INNER + GN + (g + 1) * SSM_STATE]
        c_gb = c_g.astype(BF16)
        b_gb = b_g.astype(BF16)
        cb = _dot_nt(c_gb, b_gb)
        bt = b_g.T
        g0 = g * (D_INNER // SSM_GROUPS)
        y_off_g = _dot(c_gb, st_ref[:, g0:g0 + D_INNER // SSM_GROUPS].astype(BF16))
        for jp in range(D_INNER // SSM_GROUPS // LANES):
            lo = g0 + jp * LANES
            h0 = lo // SSM_HEAD_DIM
            m_parts, mp_parts, dec_parts, ecol_parts = [], [], [], []
            for h in (h0, h0 + 1):
                colv = cs[:, h:h + 1]
                rowv = cs_t[h:h + 1, :]
                dtrow = dt_t[h:h + 1, :]
                last = cs[CHUNK - 1:CHUNK, h:h + 1]
                lmat = jnp.exp(jnp.where(causal, colv - rowv, -jnp.inf))
                m_parts.append((cb * lmat * dtrow).astype(BF16))
                mp_parts.append((bt * (jnp.exp(last - rowv) * dtrow)).astype(BF16))
                dec_parts.append(jnp.exp(last))
                ecol_parts.append(e_cs[:, h:h + 1])
            lhs = jnp.concatenate([jnp.concatenate(m_parts, axis=1),
                                   jnp.concatenate(mp_parts, axis=1)], axis=0)
            x_pair = xc_ref[:, lo:lo + LANES]
            rhs = jnp.concatenate([jnp.where(lane_lo, x_pair, 0.0).astype(BF16),
                                   jnp.where(lane_lo, 0.0, x_pair).astype(BF16)], axis=0)
            res = _dot(lhs, rhs)
            y_diag = res[0:CHUNK]
            d_state = res[CHUNK:2 * CHUNK]
            decay = jnp.where(lane_lo, dec_parts[0], dec_parts[1])
            ecol = jnp.where(lane_lo, ecol_parts[0], ecol_parts[1])
            st_ref[:, lo:lo + LANES] = st_ref[:, lo:lo + LANES] * decay + d_state
            y_pair = y_diag + y_off_g[:, jp * LANES:(jp + 1) * LANES] * ecol + x_pair * dexp_ref[:, lo:lo + LANES]
            ybuf_ref[:, lo:lo + LANES] = y_pair

    gated = ybuf_ref[...] * _silu(z_ref[...])
    ms = jnp.mean(gated * gated, axis=-1, keepdims=True)
    y_ref[...] = (gated * lax.rsqrt(ms + NORM_EPS) * nw_ref[...]).astype(y_ref.dtype)

    @pl.when(c == n_chunks - 1)
    def _():
        conv_out_ref[0] = xbc_ref[CHUNK - SUBLANES:CHUNK, :]
        st_out_ref[0] = st_ref[...]
        for jp in range(D_INNER // LANES):
            t = st_ref[:, jp * LANES:(jp + 1) * LANES].T
            ssm_out_ref[0, 2 * jp] = t[0:SSM_HEAD_DIM]
            ssm_out_ref[0, 2 * jp + 1] = t[SSM_HEAD_DIM:2 * SSM_HEAD_DIM]


def _mamba_prompt(xbc, z, dt_raw, tail0, st0, conv_w, conv_b, dt_bias, a_log, d_exp, norm_w, nb, n_chunks, pad_rows,
                  dec=None):
    m = nb * n_chunks * CHUNK
    blk = lambda b, c, *_: (b * n_chunks + c, 0)
    per_seq3 = lambda b, c, *_: (b, 0, 0)
    return _fused_call(
        functools.partial(_mamba_prompt_kernel, pad_rows=pad_rows), "mamba_prompt", (nb, n_chunks),
        ("parallel", "arbitrary"), (xbc, z, dt_raw, tail0, st0, conv_w, conv_b, dt_bias, a_log, d_exp, norm_w),
        in_specs=[pl.BlockSpec((CHUNK, CONV_DIM), blk), pl.BlockSpec((CHUNK, D_INNER), blk),
                  pl.BlockSpec((CHUNK, LANES), blk), _const_spec(tail0.shape), _const_spec(st0.shape),
                  _const_spec(conv_w.shape), _const_spec(conv_b.shape), _const_spec(dt_bias.shape),
                  _const_spec(a_log.shape), _const_spec(d_exp.shape), _const_spec(norm_w.shape)],
        out_specs=[pl.BlockSpec((CHUNK, D_INNER), blk),
                   pl.BlockSpec((1, SUBLANES, CONV_DIM), per_seq3),
                   pl.BlockSpec((1, SSM_HEADS, SSM_HEAD_DIM, SSM_STATE), lambda b, c, *_: (b, 0, 0, 0)),
                   pl.BlockSpec((1, SSM_STATE, D_INNER), per_seq3)],
        out_shape=[jax.ShapeDtypeStruct((m, D_INNER), BF16),
                   jax.ShapeDtypeStruct((nb, SUBLANES, CONV_DIM), F32),
                   jax.ShapeDtypeStruct((nb, SSM_HEADS, SSM_HEAD_DIM, SSM_STATE), F32),
                   jax.ShapeDtypeStruct((nb, SSM_STATE, D_INNER), F32)],
        scratch_shapes=[pltpu.VMEM((SUBLANES + CHUNK, CONV_DIM), F32),
                        pltpu.VMEM((SSM_STATE, D_INNER), F32),
                        pltpu.VMEM((CHUNK, CONV_DIM), F32),
                        pltpu.VMEM((CHUNK, D_INNER), F32)],
        dec=dec)


def _conv_step_kernel(xbc_ref, prev_ref, cw_ref, cb_ref, dtr_ref, dtb_ref, alog_ref,
                      xc_ref, newconv_ref, dt_ref, decay_ref):
    x_raw = xbc_ref[...]
    conv = cb_ref[...] + cw_ref[3:4, :] * x_raw
    for k in range(D_CONV - 1):
        conv = conv + cw_ref[k:k + 1, :] * prev_ref[k]
    xc_ref[...] = _silu(conv)
    newconv_ref[0] = prev_ref[1]
    newconv_ref[1] = prev_ref[2]
    newconv_ref[2] = x_raw
    dt = _softplus(dtr_ref[...] + dtb_ref[...])
    dt_ref[...] = dt
    decay_ref[...] = jnp.exp(dt * (-jnp.exp(alog_ref[...])))


def _conv_step(xbc, prev, conv_w, conv_b, dt_raw, dt_bias, a_log):
    nb = xbc.shape[0]
    return pl.pallas_call(
        _conv_step_kernel,
        out_shape=[jax.ShapeDtypeStruct((nb, CONV_DIM), F32), jax.ShapeDtypeStruct((D_CONV - 1, nb, CONV_DIM), F32),
                   jax.ShapeDtypeStruct((nb, LANES), F32), jax.ShapeDtypeStruct((nb, LANES), F32)],
        compiler_params=pltpu.CompilerParams(vmem_limit_bytes=VMEM_LIMIT_BYTES),
        name="conv_step",
    )(xbc, prev, conv_w, conv_b, dt_raw, dt_bias, a_log)


N_PAIRS = D_INNER // LANES
SSD_SEQS_PER_STEP = 2


def _ssd_step_kernel(x_ref, z_ref, dt_ref, dec_ref, b_ref, c_ref, h_ref, dsk_ref, nw_ref, y_ref, hout_ref):
    pad = jnp.zeros((LANES - N_PAIRS, LANES), F32)
    cols = lambda v: jnp.concatenate([v, pad], axis=0).T
    lane = lax.broadcasted_iota(jnp.int32, (LANES, LANES), 1)
    for s in range(x_ref.shape[0]):
        x = x_ref[s]
        x_t, dtx_t, dec_t = cols(x), cols(dt_ref[s] * x), cols(dec_ref[s])
        y_t = jnp.zeros((LANES, LANES), F32)
        for r in range(N_PAIRS):
            g = r // (N_PAIRS // SSM_GROUPS)
            b_row = b_ref[s, g:g + 1, :]
            c_row = c_ref[s, g:g + 1, :]
            dtx_col, dec_col = dtx_t[:, r:r + 1], dec_t[:, r:r + 1]
            h0 = h_ref[s, r]
            cb = jnp.sum(c_row * b_row, axis=-1, keepdims=True)
            y_col = dec_col * jnp.sum(h0 * c_row, axis=-1, keepdims=True) + cb * dtx_col
            hout_ref[s, r] = dec_col * h0 + dtx_col * b_row
            y_t = jnp.where(lane == r, y_col, y_t)
        y = y_t.T[0:N_PAIRS] + x * dsk_ref[...]
        gated = y * _silu(z_ref[s])
        ms = jnp.sum(jnp.sum(gated * gated, axis=-1, keepdims=True), axis=0, keepdims=True) / D_INNER
        y_ref[s] = (gated * lax.rsqrt(ms + NORM_EPS) * nw_ref[...]).astype(y_ref.dtype)


def _ssd_step(x, z, dt_ch, decay_ch, b3, c3, h0, d_skip, norm_w):
    nb = x.shape[0]
    ns = SSD_SEQS_PER_STEP if nb % SSD_SEQS_PER_STEP == 0 else 1
    per_b = lambda b: (b, 0, 0)
    vec = pl.BlockSpec((ns, N_PAIRS, LANES), per_b)
    pairs = lambda t: t.reshape(-1, N_PAIRS, LANES)
    st_spec = pl.BlockSpec((ns, N_PAIRS, LANES, SSM_STATE), lambda b: (b, 0, 0, 0))
    grp = pl.BlockSpec((ns, SSM_GROUPS, SSM_STATE), per_b)
    y, h_new = pl.pallas_call(
        _ssd_step_kernel,
        grid=(nb // ns,),
        in_specs=[vec, vec, vec, vec, grp, grp, st_spec, _const_spec((N_PAIRS, LANES)), _const_spec((N_PAIRS, LANES))],
        out_specs=[vec, st_spec],
        out_shape=[jax.ShapeDtypeStruct((nb, N_PAIRS, LANES), BF16),
                   jax.ShapeDtypeStruct((nb, N_PAIRS, LANES, SSM_STATE), F32)],
        compiler_params=_cparams(("parallel",)),
        name="ssd_step",
    )(pairs(x), pairs(z), pairs(dt_ch), pairs(decay_ch), b3, c3, h0.reshape(nb, N_PAIRS, LANES, SSM_STATE),
      d_skip.reshape(N_PAIRS, LANES), norm_w.reshape(N_PAIRS, LANES))
    return y.reshape(nb, D_INNER), h_new.reshape(h0.shape)


FF_CHUNK = 256


def _proj_ffn_ln_kernel(a_ref, r_ref, w_ref, mw_ref, mb_ref, wg_ref, wu_ref, wd_ref, fw_ref, fb_ref, o_ref):
    mix = _dot(a_ref[...], w_ref[...])
    x = _layer_norm(DEEPNORM_ALPHA * r_ref[...] + mix, mw_ref[...], mb_ref[...])
    xb = x.astype(BF16)
    acc = jnp.zeros(x.shape, F32)
    for f0 in range(0, D_FF, FF_CHUNK):
        gate = _dot(xb, wg_ref[:, f0:f0 + FF_CHUNK])
        up = _dot(xb, wu_ref[:, f0:f0 + FF_CHUNK])
        act = (_silu(gate) * up).astype(BF16)
        acc = acc + _dot(act, wd_ref[f0:f0 + FF_CHUNK, :])
    o_ref[...] = _layer_norm(DEEPNORM_ALPHA * x + acc, fw_ref[...], fb_ref[...])


def _proj_ffn_ln(a, resid, w, mix_w, mix_b, wg, wu, wd, ffn_w, ffn_b, tm):
    m, k = a.shape
    vec = _const_spec((1, D_MODEL))
    return _row_tiled_call(
        _proj_ffn_ln_kernel, "proj_ffn_ln", (a, resid, w, mix_w, mix_b, wg, wu, wd, ffn_w, ffn_b),
        in_specs=[pl.BlockSpec((tm, k), _ROW), pl.BlockSpec((tm, D_MODEL), _ROW), _weight_spec(w.shape), vec, vec,
                  _weight_spec(wg.shape), _weight_spec(wu.shape), _weight_spec(wd.shape), vec, vec],
        out_specs=[pl.BlockSpec((tm, D_MODEL), _ROW)],
        out_shape=[jax.ShapeDtypeStruct((m, D_MODEL), F32)],
        n_steps=m // tm)[0]


def _rotary(x, cos_t, sin_a, sin_b):
    half = ROT_DIM // 2
    parts = []
    for j in range(x.shape[1] // LANES):
        xb = x[:, j * LANES:(j + 1) * LANES]
        fwd = pltpu.roll(xb, LANES - half, 1)
        bwd = pltpu.roll(xb, half, 1)
        parts.append(xb * cos_t + fwd * sin_a + bwd * sin_b)
    return jnp.concatenate(parts, axis=1)


def _kvq_kernel(h_ref, wk_ref, wv_ref, wq_ref, cos_ref, sa_ref, sb_ref, k_ref, v_ref, kb_ref, vb_ref, qb_ref):
    hb = h_ref[...].astype(BF16)
    cos_t, sin_a, sin_b = cos_ref[...], sa_ref[...], sb_ref[...]
    k = _rotary(_dot(hb, wk_ref[...]), cos_t, sin_a, sin_b)
    k_ref[...] = k
    kb_ref[...] = k.astype(BF16)
    v = _dot(hb, wv_ref[...])
    v_ref[...] = v
    vb_ref[...] = v.astype(BF16)
    q = _rotary(_dot(hb, wq_ref[...]), cos_t, sin_a, sin_b)
    qb_ref[...] = q.astype(qb_ref.dtype)


def _kvq(h, wk, wv, wq, cos_t, sin_a, sin_b, tm, q_dtype):
    m = h.shape[0]
    tab_blocks = cos_t.shape[0] // tm
    full = pl.BlockSpec((tm, D_MODEL), _ROW)
    tab = pl.BlockSpec((tm, LANES), lambda i, *_: (i % tab_blocks, 0))
    return _row_tiled_call(
        _kvq_kernel, "kvq_proj", (h, wk, wv, wq, cos_t, sin_a, sin_b),
        in_specs=[full, _weight_spec(wk.shape), _weight_spec(wv.shape), _weight_spec(wq.shape), tab, tab, tab],
        out_specs=[full, full, full, full, full],
        out_shape=[jax.ShapeDtypeStruct((m, D_MODEL), F32), jax.ShapeDtypeStruct((m, D_MODEL), F32),
                   jax.ShapeDtypeStruct((m, D_MODEL), BF16), jax.ShapeDtypeStruct((m, D_MODEL), BF16),
                   jax.ShapeDtypeStruct((m, D_MODEL), q_dtype)],
        n_steps=m // tm)


def _kvq_prompt_kernel(h_ref, wk_ref, wv_ref, wq_ref, cos_ref, sa_ref, sb_ref, kmeta_ref, vmeta_ref,
                       kb_ref, vb_ref, qb_ref, kfull_ref, vfull_ref, kbuf, vbuf, tile_sem, meta_sem,
                       *, tiles_per_seq):
    i = pl.program_id(0)
    tm = h_ref.shape[0]
    slot = i % 2
    seq_i = i // tiles_per_seq
    row0 = N_META + (i % tiles_per_seq) * tm

    def tile_copy(buf, full, s, b, r, which):
        return pltpu.make_async_copy(buf.at[s], full.at[b, pl.ds(r, tm), :], tile_sem.at[s, which])

    @pl.when(i >= 2)
    def _():
        tile_copy(kbuf, kfull_ref, slot, seq_i, row0, 0).wait()
        tile_copy(vbuf, vfull_ref, slot, seq_i, row0, 1).wait()

    hb = h_ref[...].astype(BF16)
    cos_t, sin_a, sin_b = cos_ref[...], sa_ref[...], sb_ref[...]
    k = _rotary(_dot(hb, wk_ref[...]), cos_t, sin_a, sin_b)
    kbuf[slot] = k
    tile_copy(kbuf, kfull_ref, slot, seq_i, row0, 0).start()
    kb_ref[...] = k.astype(BF16)
    v = _dot(hb, wv_ref[...])
    vbuf[slot] = v
    tile_copy(vbuf, vfull_ref, slot, seq_i, row0, 1).start()
    vb_ref[...] = v.astype(BF16)
    q = _rotary(_dot(hb, wq_ref[...]), cos_t, sin_a, sin_b)
    qb_ref[...] = q.astype(qb_ref.dtype)

    @pl.when(i % tiles_per_seq == 0)
    def _():
        copies = [pltpu.make_async_copy(src, full.at[seq_i, pl.ds(0, N_META), :], meta_sem.at[w])
                  for w, (src, full) in enumerate(((kmeta_ref, kfull_ref), (vmeta_ref, vfull_ref)))]
        for cp in copies:
            cp.start()
        for cp in copies:
            cp.wait()

    @pl.when(i == pl.num_programs(0) - 1)
    def _():
        for s in range(2):
            tile_copy(kbuf, kfull_ref, s, seq_i, row0, 0).wait()
            tile_copy(vbuf, vfull_ref, s, seq_i, row0, 1).wait()


def _kvq_prompt(h, wk, wv, wq, cos_t, sin_a, sin_b, k_aux, v_aux, tm, nb, seq):
    m = h.shape[0]
    n_steps = m // tm
    assert n_steps >= 2 and seq % tm == 0 and PAD_ROWS % N_META == 0
    full = pl.BlockSpec((tm, D_MODEL), _ROW)
    tab = pl.BlockSpec((tm, LANES), lambda i, *_: (i % (seq // tm), 0))
    meta = pl.BlockSpec((N_META, D_MODEL), lambda i, *_: (PAD_ROWS // N_META, 0))
    hbm = pl.BlockSpec(memory_space=pl.ANY)
    seq_out = jax.ShapeDtypeStruct((nb, N_META + seq, D_MODEL), F32)
    return _fused_call(
        functools.partial(_kvq_prompt_kernel, tiles_per_seq=seq // tm), "kvq_prompt", (n_steps,), ("arbitrary",),
        (h, wk, wv, wq, cos_t, sin_a, sin_b, k_aux, v_aux),
        in_specs=[full, _weight_spec(wk.shape), _weight_spec(wv.shape), _weight_spec(wq.shape), tab, tab, tab,
                  meta, meta],
        out_specs=[full, full, full, hbm, hbm],
        out_shape=[jax.ShapeDtypeStruct((m, D_MODEL), BF16)] * 3 + [seq_out, seq_out],
        scratch_shapes=[pltpu.VMEM((2, tm, D_MODEL), F32), pltpu.VMEM((2, tm, D_MODEL), F32),
                        pltpu.SemaphoreType.DMA((2, 2)), pltpu.SemaphoreType.DMA((2,))])


def _rope_tables(pos):
    half = ROT_DIM // 2
    inv_freq = ROPE_THETA ** (-jnp.arange(half, dtype=F32) * 2.0 / ROT_DIM)
    ang = pos.astype(F32)[:, None] * inv_freq[None, :]
    cos, sin = jnp.cos(ang), jnp.sin(ang)
    n = pos.shape[0]
    rest = DIFF_QK_DIM - ROT_DIM
    cos_sub = jnp.concatenate([cos, cos, jnp.ones((n, rest), F32)], axis=1)
    sa_sub = jnp.concatenate([-sin, jnp.zeros((n, half + rest), F32)], axis=1)
    sb_sub = jnp.concatenate([jnp.zeros((n, half), F32), sin, jnp.zeros((n, rest), F32)], axis=1)
    rep = lambda t: jnp.concatenate([t, t], axis=1)
    return rep(cos_sub), rep(sa_sub), rep(sb_sub)


def _diff_lambda(lam_ref, lambda_init):
    lv = lam_ref[...]
    s1 = jnp.sum(lv[0:1] * lv[1:2], axis=-1, keepdims=True)
    s2 = jnp.sum(lv[2:3] * lv[3:4], axis=-1, keepdims=True)
    return jnp.exp(s1) - jnp.exp(s2) + lambda_init


def _fold_lanes(x, op):
    r = x[:, 0:LANES]
    for t in range(1, x.shape[1] // LANES):
        r = op(r, x[:, t * LANES:(t + 1) * LANES])
    return r


def _exp2_minus(s, m_rep):
    return jnp.concatenate([jnp.exp2(s[:, t * LANES:(t + 1) * LANES] - m_rep)
                            for t in range(s.shape[1] // LANES)], axis=1)


def _attn_prompt_kernel(q_ref, k_ref, v_ref, km_ref, vm_ref, lam_ref, sw_ref, o_ref,
                        s_ref, sm_ref, mx_ref, acc_ref, *, lambda_init, meta_pad):
    i = pl.program_id(2)
    tq = ATT_TQ
    lane = lax.broadcasted_iota(jnp.int32, (tq, LANES), 1)
    heads = [hh * LANES for hh in range(ATT_HEADS) for _ in range(2)]
    q_sub = []
    for hh in range(ATT_HEADS):
        q = q_ref[0, :, hh * LANES:(hh + 1) * LANES].astype(F32)
        q_sub.append(jnp.where(lane < DIFF_QK_DIM, q, 0.0).astype(BF16))
        q_sub.append(jnp.where(lane < DIFF_QK_DIM, 0.0, q).astype(BF16))
    n_streams = len(q_sub)

    meta_keep = lax.broadcasted_iota(jnp.int32, (tq, CHUNK), 1) >= meta_pad
    for n in range(n_streams):
        s = jnp.where(meta_keep, _dot_nt(q_sub[n], km_ref[:, heads[n]:heads[n] + LANES]), -jnp.inf)
        sm_ref[n] = s
        mx_ref[n] = s

    def qk_block(j, keep):
        rows = pl.ds(pl.multiple_of(j * tq, tq), tq)
        for n in range(n_streams):
            s = _dot_nt(q_sub[n], k_ref[0, rows, heads[n]:heads[n] + LANES])
            if keep is not None:
                s = jnp.where(keep, s, -jnp.inf)
            s_ref[n, j] = s
            mx_ref[n] = jnp.maximum(mx_ref[n], _fold_lanes(s, jnp.maximum))

    def qk_body(j, carry):
        qk_block(j, None)
        return carry

    lax.fori_loop(0, i, qk_body, 0)
    qk_block(i, lax.broadcasted_iota(jnp.int32, (tq, tq), 1) <= lax.broadcasted_iota(jnp.int32, (tq, tq), 0))

    for n in range(n_streams):
        mx_ref[n] = jnp.broadcast_to(jnp.max(mx_ref[n], axis=-1, keepdims=True), (tq, LANES))

    ones_meta = jnp.ones((CHUNK, LANES), BF16)
    for n in range(n_streams):
        p = jnp.exp2(sm_ref[n] - mx_ref[n])
        v_meta = jnp.concatenate([vm_ref[:, heads[n]:heads[n] + LANES], ones_meta], axis=1)
        acc_ref[n] = _dot(p.astype(BF16), v_meta)

    ones_blk = jnp.ones((tq, LANES), BF16)

    def pv_body(j, carry):
        rows = pl.ds(pl.multiple_of(j * tq, tq), tq)
        for n in range(n_streams):
            vb = jnp.concatenate([v_ref[0, rows, heads[n]:heads[n] + LANES], ones_blk], axis=1)
            p = _exp2_minus(s_ref[n, j], mx_ref[n])
            acc_ref[n] = acc_ref[n] + _dot(p.astype(BF16), vb)
        return carry

    lax.fori_loop(0, i + 1, pv_body, 0)

    lam = _diff_lambda(lam_ref, lambda_init)
    for hh in range(ATT_HEADS):
        acc0, acc1 = acc_ref[2 * hh], acc_ref[2 * hh + 1]
        o = acc0[:, 0:DIFF_V_DIM] / acc0[:, DIFF_V_DIM:] - lam * (acc1[:, 0:DIFF_V_DIM] / acc1[:, DIFF_V_DIM:])
        ms = jnp.mean(o * o, axis=-1, keepdims=True)
        o = o * lax.rsqrt(ms + NORM_EPS) * sw_ref[...] * (1.0 - lambda_init)
        o_ref[0, :, hh * LANES:(hh + 1) * LANES] = o.astype(o_ref.dtype)


def _attn_prompt(q3, k3, v3, k_meta, v_meta, lam_vecs, subln_w, lambda_init, meta_pad):
    nb, seq_len, _ = q3.shape
    n_q = seq_len // ATT_TQ
    kern = functools.partial(_attn_prompt_kernel, lambda_init=lambda_init, meta_pad=meta_pad)
    width = ATT_HEADS * LANES
    n_streams = 2 * ATT_HEADS
    q_spec = pl.BlockSpec((1, ATT_TQ, width), lambda b, h, i: (b, i, h))
    kv_spec = pl.BlockSpec((1, seq_len, width), lambda b, h, i: (b, 0, h))
    meta_spec = pl.BlockSpec((CHUNK, width), lambda b, h, i: (0, h))
    return pl.pallas_call(
        kern,
        grid=(nb, DIFF_HEADS // ATT_HEADS, n_q),
        in_specs=[q_spec, kv_spec, kv_spec, meta_spec, meta_spec,
                  _const_spec(lam_vecs.shape), _const_spec(subln_w.shape)],
        out_specs=q_spec,
        out_shape=jax.ShapeDtypeStruct((nb, seq_len, DIFF_HEADS * DIFF_V_DIM), BF16),
        scratch_shapes=[pltpu.VMEM((n_streams, n_q, ATT_TQ, ATT_TQ), F32),
                        pltpu.VMEM((n_streams, ATT_TQ, CHUNK), F32),
                        pltpu.VMEM((n_streams, ATT_TQ, LANES), F32),
                        pltpu.VMEM((n_streams, ATT_TQ, DIFF_V_DIM + LANES), F32)],
        compiler_params=_cparams(("parallel", "parallel", "arbitrary")),
        name="attn_prompt",
    )(q3, k3, v3, k_meta, v_meta, lam_vecs, subln_w)


N_SUB = 2 * DIFF_HEADS


def _decode_attn_init(j, scratch):
    m_ref, l_ref, acc_ref = scratch

    @pl.when(j == 0)
    def _():
        m_ref[...] = jnp.full(m_ref.shape, -jnp.inf, F32)
        l_ref[...] = jnp.zeros(l_ref.shape, F32)
        acc_ref[...] = jnp.zeros(acc_ref.shape, F32)


def _decode_query_matrix(q_ref):
    row = lax.broadcasted_iota(jnp.int32, (N_SUB, HQK), 0)
    col = lax.broadcasted_iota(jnp.int32, (N_SUB, HQK), 1)
    q_mask = (col // DIFF_QK_DIM) == (2 * (row % DIFF_HEADS) + row // DIFF_HEADS)
    return jnp.where(q_mask, q_ref[0], 0.0)


def _decode_attn_step(npg, refs, scratch):
    k_refs = refs[0:npg]
    v_refs = refs[npg:2 * npg]
    q_ref, rep_ref = refs[2 * npg], refs[2 * npg + 3]
    m_ref, l_ref, acc_ref = scratch

    row = lax.broadcasted_iota(jnp.int32, (N_SUB, HQK), 0)
    col = lax.broadcasted_iota(jnp.int32, (N_SUB, HQK), 1)
    p_mask = (col % DIFF_HEADS) == (row % DIFF_HEADS)
    q_bdb = _decode_query_matrix(q_ref).astype(BF16)

    s = [_dot(q_bdb, k_refs[u][0].astype(BF16)) for u in range(npg)]
    m_old = m_ref[...]
    m_new = m_old
    for u in range(npg):
        m_new = jnp.maximum(m_new, jnp.max(s[u], axis=-1, keepdims=True))
    alpha = jnp.exp2(m_old - m_new)
    p = [jnp.exp2(s[u] - m_new) for u in range(npg)]
    l_new = alpha * l_ref[...]
    for u in range(npg):
        l_new = l_new + jnp.sum(p[u], axis=-1, keepdims=True)
    p_rep = _dot(jnp.concatenate(p, axis=0).astype(BF16), rep_ref[...])
    acc = alpha * acc_ref[...]
    for u in range(npg):
        p_exp = jnp.where(p_mask, p_rep[u * N_SUB:(u + 1) * N_SUB], 0.0).astype(BF16)
        acc = acc + _dot(p_exp, v_refs[u][0].astype(BF16))
    m_ref[...] = m_new
    l_ref[...] = l_new
    acc_ref[...] = acc


def _decode_attn_final(j, n_steps, npg, refs, o_ref, scratch, lambda_init):
    q_ref, kn_ref, vn_ref, _, lam_ref, sw_ref = refs[2 * npg:]
    m_ref, l_ref, acc_ref = scratch

    @pl.when(j == n_steps - 1)
    def _():
        q_bd = _decode_query_matrix(q_ref)
        m_new, l_new, acc = m_ref[...], l_ref[...], acc_ref[...]
        s_new = jnp.sum(q_bd * kn_ref[0], axis=-1, keepdims=True)
        m_fin = jnp.maximum(m_new, s_new)
        a_fin = jnp.exp2(m_new - m_fin)
        p_new = jnp.exp2(s_new - m_fin)
        l_fin = a_fin * l_new + p_new
        v_new = jnp.concatenate([vn_ref[0], vn_ref[0]], axis=0)
        out = (a_fin * acc + p_new * v_new) / l_fin
        lam = _diff_lambda(lam_ref, lambda_init)
        o = out[0:DIFF_HEADS] - lam * out[DIFF_HEADS:N_SUB]
        ms = jnp.mean(o * o, axis=-1, keepdims=True)
        o_ref[0] = (o * lax.rsqrt(ms + NORM_EPS) * sw_ref[...] * (1.0 - lambda_init)).astype(o_ref.dtype)


def _fused_call(host_kernel, name, grid, semantics, args, in_specs, out_specs, out_shape, scratch_shapes=(),
                dec=None):
    if dec is None:
        return pl.pallas_call(host_kernel, grid=grid, in_specs=in_specs, out_specs=out_specs, out_shape=out_shape,
                              scratch_shapes=list(scratch_shapes), compiler_params=_cparams(semantics),
                              name=name)(*args)
    page_table, k_pages, v_pages, q, k_new, v_new, lam_vecs, subln_w, lambda_init = dec
    n_seq, n_pages = page_table.shape
    n_steps = math.prod(grid)
    spq = n_steps // n_seq
    npg = n_pages // spq
    assert spq * n_seq == n_steps and npg * spq == n_pages and v_pages.shape[1:] == k_pages.shape[1:]
    page_blk = (1,) + k_pages.shape[1:]

    def step_of(ids):
        t = ids[0]
        for size, i in zip(grid[1:], ids[1:]):
            t = t * size + i
        return t

    groups = page_table.reshape(n_steps, npg)

    def page_spec(u):
        return pl.BlockSpec(page_blk, lambda *ids_pt: (ids_pt[-1][step_of(ids_pt[:-1]), u], 0, 0))

    def seq_of(t):
        if spq & (spq - 1) == 0:
            return t >> (spq.bit_length() - 1)
        return t // spq

    per_seq = lambda *ids_pt: (seq_of(step_of(ids_pt[:-1])), 0, 0)
    const2 = lambda *_: (0, 0)
    rep = (jnp.arange(PAGE_SIZE)[:, None] == jnp.arange(PAGE_SIZE * DIFF_HEADS)[None, :] // DIFF_HEADS).astype(BF16)
    dec_specs = [page_spec(u) for u in range(npg)] + [page_spec(u) for u in range(npg)] + [
        pl.BlockSpec((1, 1, HQK), per_seq), pl.BlockSpec((1, 1, HQK), per_seq),
        pl.BlockSpec((1, DIFF_HEADS, DIFF_V_DIM), per_seq),
        pl.BlockSpec(rep.shape, const2), pl.BlockSpec(lam_vecs.shape, const2), pl.BlockSpec(subln_w.shape, const2)]
    dec_args = [k_pages] * npg + [v_pages] * npg + [q, k_new, v_new, rep, lam_vecs, subln_w]
    n_in, n_out, n_scr, n_dec = len(in_specs), len(out_specs), len(scratch_shapes), len(dec_specs)

    def kern(pt_ref, *refs):
        ins, dec_in = refs[:n_in], refs[n_in:n_in + n_dec]
        outs = refs[n_in + n_dec:n_in + n_dec + n_out]
        o_dec = refs[n_in + n_dec + n_out]
        scr = refs[n_in + n_dec + n_out + 1:n_in + n_dec + n_out + 1 + n_scr]
        dec_scr = refs[n_in + n_dec + n_out + 1 + n_scr:]
        j = step_of([pl.program_id(a) for a in range(len(grid))]) % spq
        _decode_attn_init(j, dec_scr)
        host_kernel(*ins, *outs, *scr)
        _decode_attn_step(npg, dec_in, dec_scr)
        _decode_attn_final(j, spq, npg, dec_in, o_dec, dec_scr, lambda_init)

    grid_spec = pltpu.PrefetchScalarGridSpec(
        num_scalar_prefetch=1,
        grid=grid,
        in_specs=list(in_specs) + dec_specs,
        out_specs=list(out_specs) + [pl.BlockSpec((1, DIFF_HEADS, DIFF_V_DIM), per_seq)],
        scratch_shapes=list(scratch_shapes) + [pltpu.VMEM((N_SUB, 1), F32), pltpu.VMEM((N_SUB, 1), F32),
                                               pltpu.VMEM((N_SUB, DIFF_V_DIM), F32)],
    )
    return pl.pallas_call(
        kern,
        grid_spec=grid_spec,
        out_shape=list(out_shape) + [jax.ShapeDtypeStruct((n_seq, DIFF_HEADS, DIFF_V_DIM), BF16)],
        compiler_params=_cparams(("arbitrary",) * len(grid)),
        name=name + "_dec",
    )(groups, *args, *dec_args)


def _row_tiled_call(host_kernel, name, args, in_specs, out_specs, out_shape, n_steps):
    return _fused_call(host_kernel, name, (n_steps,), ("parallel",), args, in_specs, out_specs, out_shape)


def kernel(x_prompt, x_sample, cache_k, cache_v, state_ssm, state_conv, page_table, meta_tokens, a_w_in, a_conv_w, a_conv_b, a_dt_bias, a_A_log, a_D, a_norm_w, a_w_out, kv_w_k, kv_w_v, b_w_q, b_lambda, b_subln_w, b_w_o, ffn_w_gate, ffn_w_up, ffn_w_down, ln_mix_w, ln_mix_b, ln_ffn_w, ln_ffn_b):
    bp, seq, _ = x_prompt.shape
    bs = x_sample.shape[0]
    n_pages = page_table.shape[1]
    past_len = n_pages * PAGE_SIZE
    tm_p = ROW_TILE
    lambda_init = 0.8 - 0.6 * math.exp(-0.3 * 1)
    scale = DIFF_QK_DIM ** -0.5

    w_in = a_w_in[0].astype(BF16)
    wdt = jnp.pad(w_in[:, D_INNER + CONV_DIM:], ((0, 0), (0, LANES - SSM_HEADS)))
    pad_h = lambda v: jnp.pad(v.reshape(1, SSM_HEADS), ((0, 0), (0, LANES - SSM_HEADS)))
    dt_bias, a_log = pad_h(a_dt_bias[0]), pad_h(a_A_log[0])
    conv_w, conv_b = a_conv_w[0], a_conv_b[0].reshape(1, CONV_DIM)
    d_exp = jnp.repeat(a_D[0], SSM_HEAD_DIM).reshape(1, D_INNER)
    norm_w = a_norm_w[0].reshape(1, D_INNER)
    w_out = a_w_out[0].astype(BF16)
    wk, wv = kv_w_k.astype(BF16), kv_w_v.astype(BF16)
    wq = (b_w_q[0] * (scale * LOG2_E)).astype(BF16)
    wo = b_w_o[0].astype(BF16)
    wg, wu, wd = ffn_w_gate.astype(BF16), ffn_w_up.astype(BF16), ffn_w_down.astype(BF16)
    ln = lambda t, l: t[l].reshape(1, D_MODEL)
    subln_w = b_subln_w[0].reshape(1, DIFF_V_DIM)
    lam_vecs = b_lambda[0]

    n_main = bp * seq
    n_aux = CHUNK + bs
    hm = x_prompt.reshape(n_main, D_MODEL)
    ha = jnp.concatenate([jnp.zeros((PAD_ROWS, D_MODEL), F32), meta_tokens, x_sample.reshape(bs, D_MODEL)], axis=0)
    smp = lambda t: t[CHUNK:]

    z_a, xbc_a, dtr_a = _in_proj(ha, w_in, wdt, n_aux)
    y_meta, tail_meta, _, st_meta = _mamba_prompt(
        xbc_a, z_a, dtr_a, jnp.zeros((SUBLANES, CONV_DIM), F32), jnp.zeros((SSM_STATE, D_INNER), F32),
        conv_w, conv_b, dt_bias, a_log, d_exp, norm_w, 1, 1, PAD_ROWS)
    prev = jnp.transpose(state_conv[0], (1, 0, 2))
    xc_s, conv_s, dt_s, decay_s = _conv_step(smp(xbc_a), prev, conv_w, conv_b, smp(dtr_a), dt_bias, a_log)
    per_ch = lambda t: jnp.repeat(t[:, :SSM_HEADS], SSM_HEAD_DIM, axis=1)
    y_s, ssm_s = _ssd_step(xc_s[:, :D_INNER], smp(z_a), per_ch(dt_s), per_ch(decay_s),
                           xc_s[:, D_INNER:D_INNER + GN].reshape(bs, SSM_GROUPS, SSM_STATE),
                           xc_s[:, D_INNER + GN:].reshape(bs, SSM_GROUPS, SSM_STATE),
                           state_ssm[0], d_exp, norm_w)
    y_a = jnp.concatenate([y_meta, y_s], axis=0)

    ha = _proj_ffn_ln(y_a, ha, w_out, ln(ln_mix_w, 0), ln(ln_mix_b, 0),
                      wg[0], wu[0], wd[0], ln(ln_ffn_w, 0), ln(ln_ffn_b, 0), n_aux)
    pos_a = jnp.concatenate([jnp.zeros((PAD_ROWS,), jnp.int32), jnp.arange(N_META),
                             jnp.full((bs,), past_len, jnp.int32)])
    k_a, v_a, kb_a, vb_a, q_a = _kvq(ha, wk, wv, wq, *_rope_tables(pos_a), n_aux, F32)
    hs, k_s, v_s, q_s = smp(ha), smp(k_a), smp(v_a), smp(q_a)

    n_pool = cache_k.shape[0]
    k_pages = jnp.transpose(cache_k, (0, 2, 3, 4, 1)).reshape(n_pool, HQK, PAGE_SIZE)
    v_pages = cache_v.reshape(n_pool, PAGE_SIZE * DIFF_HEADS, DIFF_V_DIM)
    dec = (page_table, k_pages, v_pages, q_s.reshape(bs, 1, HQK), k_s.reshape(bs, 1, HQK),
           v_s.reshape(bs, DIFF_HEADS, DIFF_V_DIM), lam_vecs, subln_w, lambda_init)
    z_m, xbc_m, dtr_m = _in_proj(hm, w_in, wdt, tm_p)
    y_m, conv_tail, ssm_p, _, o_s = _mamba_prompt(xbc_m, z_m, dtr_m, tail_meta[0], st_meta[0], conv_w, conv_b,
                                                  dt_bias, a_log, d_exp, norm_w, bp, seq // CHUNK, 0, dec)
    hm = _proj_ffn_ln(y_m, hm, w_out, ln(ln_mix_w, 0), ln(ln_mix_b, 0),
                      wg[0], wu[0], wd[0], ln(ln_ffn_w, 0), ln(ln_ffn_b, 0), tm_p)
    kb_m, vb_m, qb_m, k_full, v_full = _kvq_prompt(hm, wk, wv, wq, *_rope_tables(N_META + jnp.arange(seq)),
                                                   k_a, v_a, tm_p, bp, seq)

    to3 = lambda t: t.reshape(bp, seq, D_MODEL)
    o_m = _attn_prompt(to3(qb_m), to3(kb_m), to3(vb_m), kb_a, vb_a, lam_vecs, subln_w, lambda_init, PAD_ROWS)
    hm = _proj_ffn_ln(o_m.reshape(n_main, D_MODEL), hm, wo, ln(ln_mix_w, 1), ln(ln_mix_b, 1),
                      wg[1], wu[1], wd[1], ln(ln_ffn_w, 1), ln(ln_ffn_b, 1), tm_p)

    hs = _proj_ffn_ln(o_s.reshape(bs, D_MODEL), hs, wo, ln(ln_mix_w, 1), ln(ln_mix_b, 1),
                      wg[1], wu[1], wd[1], ln(ln_ffn_w, 1), ln(ln_ffn_b, 1), bs)

    y_prompt = to3(hm)
    k_prompt = k_full.reshape(bp, N_META + seq, DIFF_HEADS, 2, DIFF_QK_DIM)
    v_prompt = v_full.reshape(bp, N_META + seq, DIFF_HEADS, DIFF_V_DIM)
    ssm_prompt = ssm_p[None]
    conv_prompt = conv_tail[None, :, SUBLANES - (D_CONV - 1):]
    y_sample = hs.reshape(bs, 1, D_MODEL)
    k_sample = k_s.reshape(bs, 1, DIFF_HEADS, 2, DIFF_QK_DIM)
    v_sample = v_s.reshape(bs, 1, DIFF_HEADS, DIFF_V_DIM)
    ssm_sample = ssm_s[None]
    conv_sample = jnp.transpose(conv_s, (1, 0, 2))[None]
    return (y_prompt, y_sample, k_prompt, v_prompt, ssm_prompt, conv_prompt, k_sample, v_sample, ssm_sample,
            conv_sample)
```

```python
import functools
import math

import jax
import jax.numpy as jnp
from jax import lax
from jax.experimental import pallas as pl
from jax.experimental.pallas import tpu as pltpu

F32 = jnp.float32
BF16 = jnp.bfloat16

D_MODEL = 1024
N_META = 16
D_INNER = 2048
SSM_HEADS = 32
SSM_HEAD_DIM = 64
SSM_GROUPS = 4
SSM_STATE = 128
D_CONV = 4
GN = SSM_GROUPS * SSM_STATE
CONV_DIM = D_INNER + 2 * GN
DIFF_HEADS = 8
DIFF_QK_DIM = 64
DIFF_V_DIM = 128
HQK = DIFF_HEADS * 2 * DIFF_QK_DIM
ROT_DIM = DIFF_QK_DIM // 4
ROPE_THETA = 500000.0
D_FF = 2816
DEPTH = 2
DEEPNORM_ALPHA = (2 * DEPTH) ** 0.25
NORM_EPS = 1e-5
PAGE_SIZE = 128
LOG2_E = math.log2(math.e)

LANES = 128
SUBLANES = 8
VMEM_LIMIT_BYTES = 56 * 1024 * 1024

CHUNK = 128
CONV_COLS = 512
PAD_ROWS = CHUNK - N_META
ATT_TQ = 512
ATT_HEADS = 2
ROW_TILE = 512


def _cparams(sem):
    return pltpu.CompilerParams(dimension_semantics=sem, vmem_limit_bytes=VMEM_LIMIT_BYTES)


def _const_spec(shape):
    nd = len(shape)
    return pl.BlockSpec(shape, lambda *_: (0,) * nd)


def _weight_spec(shape):
    nd = len(shape)
    return pl.BlockSpec(shape, lambda *_: (0,) * nd, pipeline_mode=pl.Buffered(1))


def _ROW(i, *_):
    return (i, 0)


def _layer_norm(x, w, b):
    mu = jnp.mean(x, axis=-1, keepdims=True)
    xc = x - mu
    var = jnp.mean(xc * xc, axis=-1, keepdims=True)
    return xc * lax.rsqrt(var + NORM_EPS) * w + b


def _silu(x):
    h = 0.5 * x
    return h + h * jnp.tanh(h)


def _softplus(x):
    return jnp.maximum(x, 0.0) + jnp.log1p(jnp.exp(-jnp.abs(x)))


def _dot(a, b):
    return jnp.dot(a, b, preferred_element_type=F32)


def _dot_nt(a, b):
    return lax.dot_general(a, b, (((1,), (1,)), ((), ())), preferred_element_type=F32)


NCHUNK = 512


def _in_proj_kernel(x_ref, w_ref, wdt_ref, z_ref, xbc_ref, dt_ref):
    xb = x_ref[...].astype(BF16)
    for n0 in range(0, D_INNER, NCHUNK):
        z_ref[:, n0:n0 + NCHUNK] = _dot(xb, w_ref[:, n0:n0 + NCHUNK])
    for n0 in range(0, CONV_DIM, NCHUNK):
        xbc_ref[:, n0:n0 + NCHUNK] = _dot(xb, w_ref[:, D_INNER + n0:D_INNER + n0 + NCHUNK])
    dt_ref[...] = _dot(xb, wdt_ref[...])


def _in_proj(x, w, wdt, tm):
    m = x.shape[0]
    return _row_tiled_call(
        _in_proj_kernel, "in_proj", (x, w, wdt),
        in_specs=[pl.BlockSpec((tm, D_MODEL), _ROW), _weight_spec(w.shape), _weight_spec(wdt.shape)],
        out_specs=[pl.BlockSpec((tm, D_INNER), _ROW), pl.BlockSpec((tm, CONV_DIM), _ROW),
                   pl.BlockSpec((tm, LANES), _ROW)],
        out_shape=[jax.ShapeDtypeStruct((m, D_INNER), F32), jax.ShapeDtypeStruct((m, CONV_DIM), F32),
                   jax.ShapeDtypeStruct((m, LANES), F32)],
        n_steps=m // tm)


def _causal_conv_silu(ext, cw_ref, cb_ref, cols):
    prev1 = pltpu.roll(ext, 1, 0)
    u = cw_ref[1:2, cols] * ext + cw_ref[0:1, cols] * prev1
    conv = (cb_ref[:, cols] + cw_ref[3:4, cols] * ext[SUBLANES:] + cw_ref[2:3, cols] * prev1[SUBLANES:]
            + pltpu.roll(u, 2, 0)[SUBLANES:])
    return _silu(conv)


def _mamba_prompt_kernel(xbc_ref, z_ref, dtr_ref, tail0_ref, st0_ref, cw_ref, cb_ref, dtb_ref, alog_ref, dexp_ref,
                         nw_ref, y_ref, conv_out_ref, ssm_out_ref, st_out_ref,
                         ext_ref, st_ref, xc_ref, ybuf_ref, *, pad_rows):
    c = pl.program_id(1)
    n_chunks = pl.num_programs(1)

    @pl.when(c == 0)
    def _():
        ext_ref[0:SUBLANES, :] = tail0_ref[...]
        st_ref[...] = st0_ref[...]

    x_raw = xbc_ref[...]
    ext_ref[SUBLANES:SUBLANES + CHUNK, :] = x_raw
    for n0 in range(0, CONV_DIM, CONV_COLS):
        cols = slice(n0, n0 + CONV_COLS)
        xc_ref[:, cols] = _causal_conv_silu(ext_ref[:, cols], cw_ref, cb_ref, cols)
    ext_ref[0:SUBLANES, :] = x_raw[CHUNK - SUBLANES:CHUNK, :]

    row = lax.broadcasted_iota(jnp.int32, (CHUNK, LANES), 0)
    col = lax.broadcasted_iota(jnp.int32, (CHUNK, LANES), 1)
    dt = _softplus(dtr_ref[...] + dtb_ref[...])
    if pad_rows:
        dt = jnp.where(jnp.logical_or(c > 0, row >= pad_rows), dt, 0.0)
    a = dt * (-jnp.exp(alog_ref[...]))
    tri = (row >= col).astype(F32)
    cs = jnp.dot(tri, a, preferred_element_type=F32, precision=lax.Precision.HIGHEST)
    cs_t = cs.T
    dt_t = dt.T
    e_cs = jnp.exp(cs)
    causal = row >= col
    lane_lo = col < SSM_HEAD_DIM

    for g in range(SSM_GROUPS):
        b_g = xc_ref[:, D_INNER + g * SSM_STATE:D_INNER + (g + 1) * SSM_STATE]
        c_g = xc_ref[:, D_INNER + GN + g * SSM_STATE:D_INNER + GN + (g + 1) * SSM_STATE]
        c_gb = c_g.astype(BF16)
        b_gb = b_g.astype(BF16)
        cb = _dot_nt(c_gb, b_gb)
        bt = b_g.T
        g0 = g * (D_INNER // SSM_GROUPS)
        y_off_g = _dot(c_gb, st_ref[:, g0:g0 + D_INNER // SSM_GROUPS].astype(BF16))
        for jp in range(D_INNER // SSM_GROUPS // LANES):
            lo = g0 + jp * LANES
            h0 = lo // SSM_HEAD_DIM
            m_parts, mp_parts, dec_parts, ecol_parts = [], [], [], []
            for h in (h0, h0 + 1):
                colv = cs[:, h:h + 1]
                rowv = cs_t[h:h + 1, :]
                dtrow = dt_t[h:h + 1, :]
                last = cs[CHUNK - 1:CHUNK, h:h + 1]
                lmat = jnp.exp(jnp.where(causal, colv - rowv, -jnp.inf))
                m_parts.append((cb * lmat * dtrow).astype(BF16))
                mp_parts.append((bt * (jnp.exp(last - rowv) * dtrow)).astype(BF16))
                dec_parts.append(jnp.exp(last))
                ecol_parts.append(e_cs[:, h:h + 1])
            lhs = jnp.concatenate([jnp.concatenate(m_parts, axis=1),
                                   jnp.concatenate(mp_parts, axis=1)], axis=0)
            x_pair = xc_ref[:, lo:lo + LANES]
            rhs = jnp.concatenate([jnp.where(lane_lo, x_pair, 0.0).astype(BF16),
                                   jnp.where(lane_lo, 0.0, x_pair).astype(BF16)], axis=0)
            res = _dot(lhs, rhs)
            y_diag = res[0:CHUNK]
            d_state = res[CHUNK:2 * CHUNK]
            decay = jnp.where(lane_lo, dec_parts[0], dec_parts[1])
            ecol = jnp.where(lane_lo, ecol_parts[0], ecol_parts[1])
            st_ref[:, lo:lo + LANES] = st_ref[:, lo:lo + LANES] * decay + d_state
            y_pair = y_diag + y_off_g[:, jp * LANES:(jp + 1) * LANES] * ecol + x_pair * dexp_ref[:, lo:lo + LANES]
            ybuf_ref[:, lo:lo + LANES] = y_pair

    gated = ybuf_ref[...] * _silu(z_ref[...])
    ms = jnp.mean(gated * gated, axis=-1, keepdims=True)
    y_ref[...] = (gated * lax.rsqrt(ms + NORM_EPS) * nw_ref[...]).astype(y_ref.dtype)

    @pl.when(c == n_chunks - 1)
    def _():
        conv_out_ref[0] = xbc_ref[CHUNK - SUBLANES:CHUNK, :]
        st_out_ref[0] = st_ref[...]
        for jp in range(D_INNER // LANES):
            t = st_ref[:, jp * LANES:(jp + 1) * LANES].T
            ssm_out_ref[0, 2 * jp] = t[0:SSM_HEAD_DIM]
            ssm_out_ref[0, 2 * jp + 1] = t[SSM_HEAD_DIM:2 * SSM_HEAD_DIM]


def _mamba_prompt(xbc, z, dt_raw, tail0, st0, conv_w, conv_b, dt_bias, a_log, d_exp, norm_w, nb, n_chunks, pad_rows,
                  dec=None):
    m = nb * n_chunks * CHUNK
    blk = lambda b, c, *_: (b * n_chunks + c, 0)
    per_seq3 = lambda b, c, *_: (b, 0, 0)
    return _fused_call(
        functools.partial(_mamba_prompt_kernel, pad_rows=pad_rows), "mamba_prompt", (nb, n_chunks),
        ("parallel", "arbitrary"), (xbc, z, dt_raw, tail0, st0, conv_w, conv_b, dt_bias, a_log, d_exp, norm_w),
        in_specs=[pl.BlockSpec((CHUNK, CONV_DIM), blk), pl.BlockSpec((CHUNK, D_INNER), blk),
                  pl.BlockSpec((CHUNK, LANES), blk), _const_spec(tail0.shape), _const_spec(st0.shape),
                  _const_spec(conv_w.shape), _const_spec(conv_b.shape), _const_spec(dt_bias.shape),
                  _const_spec(a_log.shape), _const_spec(d_exp.shape), _const_spec(norm_w.shape)],
        out_specs=[pl.BlockSpec((CHUNK, D_INNER), blk),
                   pl.BlockSpec((1, SUBLANES, CONV_DIM), per_seq3),
                   pl.BlockSpec((1, SSM_HEADS, SSM_HEAD_DIM, SSM_STATE), lambda b, c, *_: (b, 0, 0, 0)),
                   pl.BlockSpec((1, SSM_STATE, D_INNER), per_seq3)],
        out_shape=[jax.ShapeDtypeStruct((m, D_INNER), BF16),
                   jax.ShapeDtypeStruct((nb, SUBLANES, CONV_DIM), F32),
                   jax.ShapeDtypeStruct((nb, SSM_HEADS, SSM_HEAD_DIM, SSM_STATE), F32),
                   jax.ShapeDtypeStruct((nb, SSM_STATE, D_INNER), F32)],
        scratch_shapes=[pltpu.VMEM((SUBLANES + CHUNK, CONV_DIM), F32),
                        pltpu.VMEM((SSM_STATE, D_INNER), F32),
                        pltpu.VMEM((CHUNK, CONV_DIM), F32),
                        pltpu.VMEM((CHUNK, D_INNER), F32)],
        dec=dec)


def _conv_step_kernel(xbc_ref, prev_ref, cw_ref, cb_ref, dtr_ref, dtb_ref, alog_ref,
                      xc_ref, newconv_ref, dt_ref, decay_ref):
    x_raw = xbc_ref[...]
    conv = cb_ref[...] + cw_ref[3:4, :] * x_raw
    for k in range(D_CONV - 1):
        conv = conv + cw_ref[k:k + 1, :] * prev_ref[k]
    xc_ref[...] = _silu(conv)
    newconv_ref[0] = prev_ref[1]
    newconv_ref[1] = prev_ref[2]
    newconv_ref[2] = x_raw
    dt = _softplus(dtr_ref[...] + dtb_ref[...])
    dt_ref[...] = dt
    decay_ref[...] = jnp.exp(dt * (-jnp.exp(alog_ref[...])))


def _conv_step(xbc, prev, conv_w, conv_b, dt_raw, dt_bias, a_log):
    nb = xbc.shape[0]
    return pl.pallas_call(
        _conv_step_kernel,
        out_shape=[jax.ShapeDtypeStruct((nb, CONV_DIM), F32), jax.ShapeDtypeStruct((D_CONV - 1, nb, CONV_DIM), F32),
                   jax.ShapeDtypeStruct((nb, LANES), F32), jax.ShapeDtypeStruct((nb, LANES), F32)],
        compiler_params=pltpu.CompilerParams(vmem_limit_bytes=VMEM_LIMIT_BYTES),
        name="conv_step",
    )(xbc, prev, conv_w, conv_b, dt_raw, dt_bias, a_log)


N_PAIRS = D_INNER // LANES
SSD_SEQS_PER_STEP = 2


def _ssd_step_kernel(x_ref, z_ref, dt_ref, dec_ref, b_ref, c_ref, h_ref, dsk_ref, nw_ref, y_ref, hout_ref):
    pad = jnp.zeros((LANES - N_PAIRS, LANES), F32)
    cols = lambda v: jnp.concatenate([v, pad], axis=0).T
    lane = lax.broadcasted_iota(jnp.int32, (LANES, LANES), 1)
    for s in range(x_ref.shape[0]):
        x = x_ref[s]
        x_t, dtx_t, dec_t = cols(x), cols(dt_ref[s] * x), cols(dec_ref[s])
        y_t = jnp.zeros((LANES, LANES), F32)
        for r in range(N_PAIRS):
            g = r // (N_PAIRS // SSM_GROUPS)
            b_row = b_ref[s, g:g + 1, :]
            c_row = c_ref[s, g:g + 1, :]
            dtx_col, dec_col = dtx_t[:, r:r + 1], dec_t[:, r:r + 1]
            h0 = h_ref[s, r]
            cb = jnp.sum(c_row * b_row, axis=-1, keepdims=True)
            y_col = dec_col * jnp.sum(h0 * c_row, axis=-1, keepdims=True) + cb * dtx_col
            hout_ref[s, r] = dec_col * h0 + dtx_col * b_row
            y_t = jnp.where(lane == r, y_col, y_t)
        y = y_t.T[0:N_PAIRS] + x * dsk_ref[...]
        gated = y * _silu(z_ref[s])
        ms = jnp.sum(jnp.sum(gated * gated, axis=-1, keepdims=True), axis=0, keepdims=True) / D_INNER
        y_ref[s] = (gated * lax.rsqrt(ms + NORM_EPS) * nw_ref[...]).astype(y_ref.dtype)


def _ssd_step(x, z, dt_ch, decay_ch, b3, c3, h0, d_skip, norm_w):
    nb = x.shape[0]
    ns = SSD_SEQS_PER_STEP if nb % SSD_SEQS_PER_STEP == 0 else 1
    per_b = lambda b: (b, 0, 0)
    vec = pl.BlockSpec((ns, N_PAIRS, LANES), per_b)
    pairs = lambda t: t.reshape(-1, N_PAIRS, LANES)
    st_spec = pl.BlockSpec((ns, N_PAIRS, LANES, SSM_STATE), lambda b: (b, 0, 0, 0))
    grp = pl.BlockSpec((ns, SSM_GROUPS, SSM_STATE), per_b)
    y, h_new = pl.pallas_call(
        _ssd_step_kernel,
        grid=(nb // ns,),
        in_specs=[vec, vec, vec, vec, grp, grp, st_spec, _const_spec((N_PAIRS, LANES)), _const_spec((N_PAIRS, LANES))],
        out_specs=[vec, st_spec],
        out_shape=[jax.ShapeDtypeStruct((nb, N_PAIRS, LANES), BF16),
                   jax.ShapeDtypeStruct((nb, N_PAIRS, LANES, SSM_STATE), F32)],
        compiler_params=_cparams(("parallel",)),
        name="ssd_step",
    )(pairs(x), pairs(z), pairs(dt_ch), pairs(decay_ch), b3, c3, h0.reshape(nb, N_PAIRS, LANES, SSM_STATE),
      d_skip.reshape(N_PAIRS, LANES), norm_w.reshape(N_PAIRS, LANES))
    return y.reshape(nb, D_INNER), h_new.reshape(h0.shape)


FF_CHUNK = 256


def _proj_ffn_ln_kernel(a_ref, r_ref, w_ref, mw_ref, mb_ref, wg_ref, wu_ref, wd_ref, fw_ref, fb_ref, o_ref):
    mix = _dot(a_ref[...], w_ref[...])
    x = _layer_norm(DEEPNORM_ALPHA * r_ref[...] + mix, mw_ref[...], mb_ref[...])
    xb = x.astype(BF16)
    acc = jnp.zeros(x.shape, F32)
    for f0 in range(0, D_FF, FF_CHUNK):
        gate = _dot(xb, wg_ref[0, :, f0:f0 + FF_CHUNK])
        up = _dot(xb, wu_ref[0, :, f0:f0 + FF_CHUNK])
        act = (_silu(gate) * up).astype(BF16)
        acc = acc + _dot(act, wd_ref[0, f0:f0 + FF_CHUNK, :])
    o_ref[...] = _layer_norm(DEEPNORM_ALPHA * x + acc, fw_ref[...], fb_ref[...])


def _proj_ffn_ln(a, resid, w, mix_w, mix_b, wg, wu, wd, layer, ffn_w, ffn_b, tm):
    m, k = a.shape
    vec = _const_spec((1, D_MODEL))

    def layer_spec(stacked):
        return pl.BlockSpec((1,) + stacked.shape[1:], lambda *_: (layer, 0, 0), pipeline_mode=pl.Buffered(1))

    return _row_tiled_call(
        _proj_ffn_ln_kernel, "proj_ffn_ln", (a, resid, w, mix_w, mix_b, wg, wu, wd, ffn_w, ffn_b),
        in_specs=[pl.BlockSpec((tm, k), _ROW), pl.BlockSpec((tm, D_MODEL), _ROW), _weight_spec(w.shape), vec, vec,
                  layer_spec(wg), layer_spec(wu), layer_spec(wd), vec, vec],
        out_specs=[pl.BlockSpec((tm, D_MODEL), _ROW)],
        out_shape=[jax.ShapeDtypeStruct((m, D_MODEL), F32)],
        n_steps=m // tm)[0]


def _rotary(x, cos_t, sin_a, sin_b):
    half = ROT_DIM // 2
    parts = []
    for j in range(x.shape[1] // LANES):
        xb = x[:, j * LANES:(j + 1) * LANES]
        fwd = pltpu.roll(xb, LANES - half, 1)
        bwd = pltpu.roll(xb, half, 1)
        parts.append(xb * cos_t + fwd * sin_a + bwd * sin_b)
    return jnp.concatenate(parts, axis=1)


def _kvq_kernel(h_ref, wk_ref, wv_ref, wq_ref, cos_ref, sa_ref, sb_ref, k_ref, v_ref, kb_ref, vb_ref, qb_ref):
    hb = h_ref[...].astype(BF16)
    cos_t, sin_a, sin_b = cos_ref[...], sa_ref[...], sb_ref[...]
    k = _rotary(_dot(hb, wk_ref[...]), cos_t, sin_a, sin_b)
    k_ref[...] = k
    kb_ref[...] = k.astype(BF16)
    v = _dot(hb, wv_ref[...])
    v_ref[...] = v
    vb_ref[...] = v.astype(BF16)
    q = _rotary(_dot(hb, wq_ref[...]), cos_t, sin_a, sin_b)
    qb_ref[...] = q.astype(qb_ref.dtype)


def _kvq(h, wk, wv, wq, cos_t, sin_a, sin_b, tm, q_dtype):
    m = h.shape[0]
    tab_blocks = cos_t.shape[0] // tm
    full = pl.BlockSpec((tm, D_MODEL), _ROW)
    tab = pl.BlockSpec((tm, LANES), lambda i, *_: (i % tab_blocks, 0))
    return _row_tiled_call(
        _kvq_kernel, "kvq_proj", (h, wk, wv, wq, cos_t, sin_a, sin_b),
        in_specs=[full, _weight_spec(wk.shape), _weight_spec(wv.shape), _weight_spec(wq.shape), tab, tab, tab],
        out_specs=[full, full, full, full, full],
        out_shape=[jax.ShapeDtypeStruct((m, D_MODEL), F32), jax.ShapeDtypeStruct((m, D_MODEL), F32),
                   jax.ShapeDtypeStruct((m, D_MODEL), BF16), jax.ShapeDtypeStruct((m, D_MODEL), BF16),
                   jax.ShapeDtypeStruct((m, D_MODEL), q_dtype)],
        n_steps=m // tm)


def _kvq_prompt_kernel(h_ref, wk_ref, wv_ref, wq_ref, cos_ref, sa_ref, sb_ref, kmeta_ref, vmeta_ref,
                       kb_ref, vb_ref, qb_ref, kfull_ref, vfull_ref, kbuf, vbuf, tile_sem, meta_sem,
                       *, tiles_per_seq):
    i = pl.program_id(0)
    tm = h_ref.shape[0]
    slot = i % 2
    seq_i = i // tiles_per_seq
    row0 = N_META + (i % tiles_per_seq) * tm

    def tile_copy(buf, full, s, b, r, which):
        return pltpu.make_async_copy(buf.at[s], full.at[b, pl.ds(r, tm), :], tile_sem.at[s, which])

    @pl.when(i >= 2)
    def _():
        tile_copy(kbuf, kfull_ref, slot, seq_i, row0, 0).wait()
        tile_copy(vbuf, vfull_ref, slot, seq_i, row0, 1).wait()

    hb = h_ref[...].astype(BF16)
    cos_t, sin_a, sin_b = cos_ref[...], sa_ref[...], sb_ref[...]
    k = _rotary(_dot(hb, wk_ref[...]), cos_t, sin_a, sin_b)
    kbuf[slot] = k
    tile_copy(kbuf, kfull_ref, slot, seq_i, row0, 0).start()
    kb_ref[...] = k.astype(BF16)
    v = _dot(hb, wv_ref[...])
    vbuf[slot] = v
    tile_copy(vbuf, vfull_ref, slot, seq_i, row0, 1).start()
    vb_ref[...] = v.astype(BF16)
    q = _rotary(_dot(hb, wq_ref[...]), cos_t, sin_a, sin_b)
    qb_ref[...] = q.astype(qb_ref.dtype)

    @pl.when(i % tiles_per_seq == 0)
    def _():
        copies = [pltpu.make_async_copy(src, full.at[seq_i, pl.ds(0, N_META), :], meta_sem.at[w])
                  for w, (src, full) in enumerate(((kmeta_ref, kfull_ref), (vmeta_ref, vfull_ref)))]
        for cp in copies:
            cp.start()
        for cp in copies:
            cp.wait()

    @pl.when(i == pl.num_programs(0) - 1)
    def _():
        for s in range(2):
            tile_copy(kbuf, kfull_ref, s, seq_i, row0, 0).wait()
            tile_copy(vbuf, vfull_ref, s, seq_i, row0, 1).wait()


def _kvq_prompt(h, wk, wv, wq, cos_t, sin_a, sin_b, k_aux, v_aux, tm, nb, seq):
    m = h.shape[0]
    n_steps = m // tm
    assert n_steps >= 2 and seq % tm == 0 and PAD_ROWS % N_META == 0
    full = pl.BlockSpec((tm, D_MODEL), _ROW)
    tab = pl.BlockSpec((tm, LANES), lambda i, *_: (i % (seq // tm), 0))
    meta = pl.BlockSpec((N_META, D_MODEL), lambda i, *_: (PAD_ROWS // N_META, 0))
    hbm = pl.BlockSpec(memory_space=pl.ANY)
    seq_out = jax.ShapeDtypeStruct((nb, N_META + seq, D_MODEL), F32)
    return _fused_call(
        functools.partial(_kvq_prompt_kernel, tiles_per_seq=seq // tm), "kvq_prompt", (n_steps,), ("arbitrary",),
        (h, wk, wv, wq, cos_t, sin_a, sin_b, k_aux, v_aux),
        in_specs=[full, _weight_spec(wk.shape), _weight_spec(wv.shape), _weight_spec(wq.shape), tab, tab, tab,
                  meta, meta],
        out_specs=[full, full, full, hbm, hbm],
        out_shape=[jax.ShapeDtypeStruct((m, D_MODEL), BF16)] * 3 + [seq_out, seq_out],
        scratch_shapes=[pltpu.VMEM((2, tm, D_MODEL), F32), pltpu.VMEM((2, tm, D_MODEL), F32),
                        pltpu.SemaphoreType.DMA((2, 2)), pltpu.SemaphoreType.DMA((2,))])


def _rope_tables(pos):
    half = ROT_DIM // 2
    inv_freq = ROPE_THETA ** (-jnp.arange(half, dtype=F32) * 2.0 / ROT_DIM)
    ang = pos.astype(F32)[:, None] * inv_freq[None, :]
    cos, sin = jnp.cos(ang), jnp.sin(ang)
    n = pos.shape[0]
    rest = DIFF_QK_DIM - ROT_DIM
    cos_sub = jnp.concatenate([cos, cos, jnp.ones((n, rest), F32)], axis=1)
    sa_sub = jnp.concatenate([-sin, jnp.zeros((n, half + rest), F32)], axis=1)
    sb_sub = jnp.concatenate([jnp.zeros((n, half), F32), sin, jnp.zeros((n, rest), F32)], axis=1)
    rep = lambda t: jnp.concatenate([t, t], axis=1)
    return rep(cos_sub), rep(sa_sub), rep(sb_sub)


def _diff_lambda(lam_ref, lambda_init):
    lv = lam_ref[...]
    s1 = jnp.sum(lv[0:1] * lv[1:2], axis=-1, keepdims=True)
    s2 = jnp.sum(lv[2:3] * lv[3:4], axis=-1, keepdims=True)
    return jnp.exp(s1) - jnp.exp(s2) + lambda_init


def _fold_lanes(x, op):
    r = x[:, 0:LANES]
    for t in range(1, x.shape[1] // LANES):
        r = op(r, x[:, t * LANES:(t + 1) * LANES])
    return r


def _exp2_minus(s, m_rep):
    return jnp.concatenate([jnp.exp2(s[:, t * LANES:(t + 1) * LANES] - m_rep)
                            for t in range(s.shape[1] // LANES)], axis=1)


def _attn_prompt_kernel(q_ref, k_ref, v_ref, km_ref, vm_ref, lam_ref, sw_ref, o_ref,
                        s_ref, sm_ref, mx_ref, acc_ref, *, lambda_init, meta_pad):
    i = pl.program_id(2)
    tq = ATT_TQ
    lane = lax.broadcasted_iota(jnp.int32, (tq, LANES), 1)
    heads = [hh * LANES for hh in range(ATT_HEADS) for _ in range(2)]
    q_sub = []
    for hh in range(ATT_HEADS):
        q = q_ref[0, :, hh * LANES:(hh + 1) * LANES].astype(F32)
        q_sub.append(jnp.where(lane < DIFF_QK_DIM, q, 0.0).astype(BF16))
        q_sub.append(jnp.where(lane < DIFF_QK_DIM, 0.0, q).astype(BF16))
    n_streams = len(q_sub)

    meta_keep = lax.broadcasted_iota(jnp.int32, (tq, CHUNK), 1) >= meta_pad
    for n in range(n_streams):
        s = jnp.where(meta_keep, _dot_nt(q_sub[n], km_ref[:, heads[n]:heads[n] + LANES]), -jnp.inf)
        sm_ref[n] = s
        mx_ref[n] = s

    def qk_block(j, keep):
        rows = pl.ds(pl.multiple_of(j * tq, tq), tq)
        for n in range(n_streams):
            s = _dot_nt(q_sub[n], k_ref[0, rows, heads[n]:heads[n] + LANES])
            if keep is not None:
                s = jnp.where(keep, s, -jnp.inf)
            s_ref[n, j] = s
            mx_ref[n] = jnp.maximum(mx_ref[n], _fold_lanes(s, jnp.maximum))

    def qk_body(j, carry):
        qk_block(j, None)
        return carry

    lax.fori_loop(0, i, qk_body, 0)
    qk_block(i, lax.broadcasted_iota(jnp.int32, (tq, tq), 1) <= lax.broadcasted_iota(jnp.int32, (tq, tq), 0))

    for n in range(n_streams):
        mx_ref[n] = jnp.broadcast_to(jnp.max(mx_ref[n], axis=-1, keepdims=True), (tq, LANES))

    ones_meta = jnp.ones((CHUNK, LANES), BF16)
    for n in range(n_streams):
        p = jnp.exp2(sm_ref[n] - mx_ref[n])
        v_meta = jnp.concatenate([vm_ref[:, heads[n]:heads[n] + LANES], ones_meta], axis=1)
        acc_ref[n] = _dot(p.astype(BF16), v_meta)

    ones_blk = jnp.ones((tq, LANES), BF16)

    def pv_body(j, carry):
        rows = pl.ds(pl.multiple_of(j * tq, tq), tq)
        for n in range(n_streams):
            vb = jnp.concatenate([v_ref[0, rows, heads[n]:heads[n] + LANES], ones_blk], axis=1)
            p = _exp2_minus(s_ref[n, j], mx_ref[n])
            acc_ref[n] = acc_ref[n] + _dot(p.astype(BF16), vb)
        return carry

    lax.fori_loop(0, i + 1, pv_body, 0)

    lam = _diff_lambda(lam_ref, lambda_init)
    for hh in range(ATT_HEADS):
        acc0, acc1 = acc_ref[2 * hh], acc_ref[2 * hh + 1]
        o = acc0[:, 0:DIFF_V_DIM] / acc0[:, DIFF_V_DIM:] - lam * (acc1[:, 0:DIFF_V_DIM] / acc1[:, DIFF_V_DIM:])
        ms = jnp.mean(o * o, axis=-1, keepdims=True)
        o = o * lax.rsqrt(ms + NORM_EPS) * sw_ref[...] * (1.0 - lambda_init)
        o_ref[0, :, hh * LANES:(hh + 1) * LANES] = o.astype(o_ref.dtype)


def _attn_prompt(q3, k3, v3, k_meta, v_meta, lam_vecs, subln_w, lambda_init, meta_pad):
    nb, seq_len, _ = q3.shape
    n_q = seq_len // ATT_TQ
    kern = functools.partial(_attn_prompt_kernel, lambda_init=lambda_init, meta_pad=meta_pad)
    width = ATT_HEADS * LANES
    n_streams = 2 * ATT_HEADS
    q_spec = pl.BlockSpec((1, ATT_TQ, width), lambda b, h, i: (b, i, h))
    kv_spec = pl.BlockSpec((1, seq_len, width), lambda b, h, i: (b, 0, h))
    meta_spec = pl.BlockSpec((CHUNK, width), lambda b, h, i: (0, h))
    return pl.pallas_call(
        kern,
        grid=(nb, DIFF_HEADS // ATT_HEADS, n_q),
        in_specs=[q_spec, kv_spec, kv_spec, meta_spec, meta_spec,
                  _const_spec(lam_vecs.shape), _const_spec(subln_w.shape)],
        out_specs=q_spec,
        out_shape=jax.ShapeDtypeStruct((nb, seq_len, DIFF_HEADS * DIFF_V_DIM), BF16),
        scratch_shapes=[pltpu.VMEM((n_streams, n_q, ATT_TQ, ATT_TQ), F32),
                        pltpu.VMEM((n_streams, ATT_TQ, CHUNK), F32),
                        pltpu.VMEM((n_streams, ATT_TQ, LANES), F32),
                        pltpu.VMEM((n_streams, ATT_TQ, DIFF_V_DIM + LANES), F32)],
        compiler_params=_cparams(("parallel", "parallel", "arbitrary")),
        name="attn_prompt",
    )(q3, k3, v3, k_meta, v_meta, lam_vecs, subln_w)


N_SUB = 2 * DIFF_HEADS


def _decode_attn_init(j, scratch):
    m_ref, l_ref, acc_ref = scratch

    @pl.when(j == 0)
    def _():
        m_ref[...] = jnp.full(m_ref.shape, -jnp.inf, F32)
        l_ref[...] = jnp.zeros(l_ref.shape, F32)
        acc_ref[...] = jnp.zeros(acc_ref.shape, F32)


def _decode_query_matrix(q_ref):
    row = lax.broadcasted_iota(jnp.int32, (N_SUB, HQK), 0)
    col = lax.broadcasted_iota(jnp.int32, (N_SUB, HQK), 1)
    q_mask = (col // DIFF_QK_DIM) == (2 * (row % DIFF_HEADS) + row // DIFF_HEADS)
    return jnp.where(q_mask, q_ref[0], 0.0)


def _decode_attn_step(npg, refs, scratch):
    k_refs = refs[0:npg]
    v_refs = refs[npg:2 * npg]
    q_ref, rep_ref = refs[2 * npg], refs[2 * npg + 3]
    m_ref, l_ref, acc_ref = scratch

    row = lax.broadcasted_iota(jnp.int32, (N_SUB, HQK), 0)
    col = lax.broadcasted_iota(jnp.int32, (N_SUB, HQK), 1)
    p_mask = (col % DIFF_HEADS) == (row % DIFF_HEADS)
    q_bdb = _decode_query_matrix(q_ref).astype(BF16)

    s = [_dot(q_bdb, k_refs[u][0].astype(BF16)) for u in range(npg)]
    m_old = m_ref[...]
    m_new = m_old
    for u in range(npg):
        m_new = jnp.maximum(m_new, jnp.max(s[u], axis=-1, keepdims=True))
    alpha = jnp.exp2(m_old - m_new)
    p = [jnp.exp2(s[u] - m_new) for u in range(npg)]
    l_new = alpha * l_ref[...]
    for u in range(npg):
        l_new = l_new + jnp.sum(p[u], axis=-1, keepdims=True)
    p_rep = _dot(jnp.concatenate(p, axis=0).astype(BF16), rep_ref[...])
    acc = alpha * acc_ref[...]
    for u in range(npg):
        p_exp = jnp.where(p_mask, p_rep[u * N_SUB:(u + 1) * N_SUB], 0.0).astype(BF16)
        acc = acc + _dot(p_exp, v_refs[u][0].astype(BF16))
    m_ref[...] = m_new
    l_ref[...] = l_new
    acc_ref[...] = acc


def _decode_attn_final(j, n_steps, npg, refs, o_ref, scratch, lambda_init):
    q_ref, kn_ref, vn_ref, _, lam_ref, sw_ref = refs[2 * npg:]
    m_ref, l_ref, acc_ref = scratch

    @pl.when(j == n_steps - 1)
    def _():
        q_bd = _decode_query_matrix(q_ref)
        m_new, l_new, acc = m_ref[...], l_ref[...], acc_ref[...]
        s_new = jnp.sum(q_bd * kn_ref[0], axis=-1, keepdims=True)
        m_fin = jnp.maximum(m_new, s_new)
        a_fin = jnp.exp2(m_new - m_fin)
        p_new = jnp.exp2(s_new - m_fin)
        l_fin = a_fin * l_new + p_new
        v_new = jnp.concatenate([vn_ref[0], vn_ref[0]], axis=0)
        out = (a_fin * acc + p_new * v_new) / l_fin
        lam = _diff_lambda(lam_ref, lambda_init)
        o = out[0:DIFF_HEADS] - lam * out[DIFF_HEADS:N_SUB]
        ms = jnp.mean(o * o, axis=-1, keepdims=True)
        o_ref[0] = (o * lax.rsqrt(ms + NORM_EPS) * sw_ref[...] * (1.0 - lambda_init)).astype(o_ref.dtype)


def _fused_call(host_kernel, name, grid, semantics, args, in_specs, out_specs, out_shape, scratch_shapes=(),
                dec=None):
    if dec is None:
        return pl.pallas_call(host_kernel, grid=grid, in_specs=in_specs, out_specs=out_specs, out_shape=out_shape,
                              scratch_shapes=list(scratch_shapes), compiler_params=_cparams(semantics),
                              name=name)(*args)
    page_table, k_pages, v_pages, q, k_new, v_new, lam_vecs, subln_w, lambda_init = dec
    n_seq, n_pages = page_table.shape
    n_steps = math.prod(grid)
    spq = n_steps // n_seq
    npg = n_pages // spq
    assert spq * n_seq == n_steps and npg * spq == n_pages and v_pages.shape[1:] == k_pages.shape[1:]
    page_blk = (1,) + k_pages.shape[1:]

    def step_of(ids):
        t = ids[0]
        for size, i in zip(grid[1:], ids[1:]):
            t = t * size + i
        return t

    groups = page_table.reshape(n_steps, npg)

    def page_spec(u):
        return pl.BlockSpec(page_blk, lambda *ids_pt: (ids_pt[-1][step_of(ids_pt[:-1]), u], 0, 0))

    def seq_of(t):
        if spq & (spq - 1) == 0:
            return t >> (spq.bit_length() - 1)
        return t // spq

    per_seq = lambda *ids_pt: (seq_of(step_of(ids_pt[:-1])), 0, 0)
    const2 = lambda *_: (0, 0)
    rep = (jnp.arange(PAGE_SIZE)[:, None] == jnp.arange(PAGE_SIZE * DIFF_HEADS)[None, :] // DIFF_HEADS).astype(BF16)
    dec_specs = [page_spec(u) for u in range(npg)] + [page_spec(u) for u in range(npg)] + [
        pl.BlockSpec((1, 1, HQK), per_seq), pl.BlockSpec((1, 1, HQK), per_seq),
        pl.BlockSpec((1, DIFF_HEADS, DIFF_V_DIM), per_seq),
        pl.BlockSpec(rep.shape, const2), pl.BlockSpec(lam_vecs.shape, const2), pl.BlockSpec(subln_w.shape, const2)]
    dec_args = [k_pages] * npg + [v_pages] * npg + [q, k_new, v_new, rep, lam_vecs, subln_w]
    n_in, n_out, n_scr, n_dec = len(in_specs), len(out_specs), len(scratch_shapes), len(dec_specs)

    def kern(pt_ref, *refs):
        ins, dec_in = refs[:n_in], refs[n_in:n_in + n_dec]
        outs = refs[n_in + n_dec:n_in + n_dec + n_out]
        o_dec = refs[n_in + n_dec + n_out]
        scr = refs[n_in + n_dec + n_out + 1:n_in + n_dec + n_out + 1 + n_scr]
        dec_scr = refs[n_in + n_dec + n_out + 1 + n_scr:]
        j = step_of([pl.program_id(a) for a in range(len(grid))]) % spq
        _decode_attn_init(j, dec_scr)
        host_kernel(*ins, *outs, *scr)
        _decode_attn_step(npg, dec_in, dec_scr)
        _decode_attn_final(j, spq, npg, dec_in, o_dec, dec_scr, lambda_init)

    grid_spec = pltpu.PrefetchScalarGridSpec(
        num_scalar_prefetch=1,
        grid=grid,
        in_specs=list(in_specs) + dec_specs,
        out_specs=list(out_specs) + [pl.BlockSpec((1, DIFF_HEADS, DIFF_V_DIM), per_seq)],
        scratch_shapes=list(scratch_shapes) + [pltpu.VMEM((N_SUB, 1), F32), pltpu.VMEM((N_SUB, 1), F32),
                                               pltpu.VMEM((N_SUB, DIFF_V_DIM), F32)],
    )
    return pl.pallas_call(
        kern,
        grid_spec=grid_spec,
        out_shape=list(out_shape) + [jax.ShapeDtypeStruct((n_seq, DIFF_HEADS, DIFF_V_DIM), BF16)],
        compiler_params=_cparams(("arbitrary",) * len(grid)),
        name=name + "_dec",
    )(groups, *args, *dec_args)


def _row_tiled_call(host_kernel, name, args, in_specs, out_specs, out_shape, n_steps):
    return _fused_call(host_kernel, name, (n_steps,), ("parallel",), args, in_specs, out_specs, out_shape)


def kernel(x_prompt, x_sample, cache_k, cache_v, state_ssm, state_conv, page_table, meta_tokens, a_w_in, a_conv_w, a_conv_b, a_dt_bias, a_A_log, a_D, a_norm_w, a_w_out, kv_w_k, kv_w_v, b_w_q, b_lambda, b_subln_w, b_w_o, ffn_w_gate, ffn_w_up, ffn_w_down, ln_mix_w, ln_mix_b, ln_ffn_w, ln_ffn_b):
    bp, seq, _ = x_prompt.shape
    bs = x_sample.shape[0]
    n_pages = page_table.shape[1]
    past_len = n_pages * PAGE_SIZE
    tm_p = ROW_TILE
    lambda_init = 0.8 - 0.6 * math.exp(-0.3 * 1)
    scale = DIFF_QK_DIM ** -0.5

    w_in = a_w_in[0].astype(BF16)
    wdt = jnp.pad(w_in[:, D_INNER + CONV_DIM:], ((0, 0), (0, LANES - SSM_HEADS)))
    pad_h = lambda v: jnp.pad(v.reshape(1, SSM_HEADS), ((0, 0), (0, LANES - SSM_HEADS)))
    dt_bias, a_log = pad_h(a_dt_bias[0]), pad_h(a_A_log[0])
    conv_w, conv_b = a_conv_w[0], a_conv_b[0].reshape(1, CONV_DIM)
    d_exp = jnp.repeat(a_D[0], SSM_HEAD_DIM).reshape(1, D_INNER)
    norm_w = a_norm_w[0].reshape(1, D_INNER)
    w_out = a_w_out[0].astype(BF16)
    wk, wv = kv_w_k.astype(BF16), kv_w_v.astype(BF16)
    wq = (b_w_q[0] * (scale * LOG2_E)).astype(BF16)
    wo = b_w_o[0].astype(BF16)
    wg, wu, wd = ffn_w_gate.astype(BF16), ffn_w_up.astype(BF16), ffn_w_down.astype(BF16)
    ln = lambda t, l: t[l].reshape(1, D_MODEL)
    subln_w = b_subln_w[0].reshape(1, DIFF_V_DIM)
    lam_vecs = b_lambda[0]

    n_main = bp * seq
    n_aux = CHUNK + bs
    hm = x_prompt.reshape(n_main, D_MODEL)
    ha = jnp.concatenate([jnp.zeros((PAD_ROWS, D_MODEL), F32), meta_tokens, x_sample.reshape(bs, D_MODEL)], axis=0)
    smp = lambda t: t[CHUNK:]

    z_a, xbc_a, dtr_a = _in_proj(ha, w_in, wdt, n_aux)
    y_meta, tail_meta, _, st_meta = _mamba_prompt(
        xbc_a, z_a, dtr_a, jnp.zeros((SUBLANES, CONV_DIM), F32), jnp.zeros((SSM_STATE, D_INNER), F32),
        conv_w, conv_b, dt_bias, a_log, d_exp, norm_w, 1, 1, PAD_ROWS)
    prev = jnp.transpose(state_conv[0], (1, 0, 2))
    xc_s, conv_s, dt_s, decay_s = _conv_step(smp(xbc_a), prev, conv_w, conv_b, smp(dtr_a), dt_bias, a_log)
    per_ch = lambda t: jnp.repeat(t[:, :SSM_HEADS], SSM_HEAD_DIM, axis=1)
    y_s, ssm_s = _ssd_step(xc_s[:, :D_INNER], smp(z_a), per_ch(dt_s), per_ch(decay_s),
                           xc_s[:, D_INNER:D_INNER + GN].reshape(bs, SSM_GROUPS, SSM_STATE),
                           xc_s[:, D_INNER + GN:].reshape(bs, SSM_GROUPS, SSM_STATE),
                           state_ssm[0], d_exp, norm_w)
    y_a = jnp.concatenate([y_meta, y_s], axis=0)

    ha = _proj_ffn_ln(y_a, ha, w_out, ln(ln_mix_w, 0), ln(ln_mix_b, 0),
                      wg, wu, wd, 0, ln(ln_ffn_w, 0), ln(ln_ffn_b, 0), n_aux)
    pos_a = jnp.concatenate([jnp.zeros((PAD_ROWS,), jnp.int32), jnp.arange(N_META),
                             jnp.full((bs,), past_len, jnp.int32)])
    k_a, v_a, kb_a, vb_a, q_a = _kvq(ha, wk, wv, wq, *_rope_tables(pos_a), n_aux, F32)
    hs, k_s, v_s, q_s = smp(ha), smp(k_a), smp(v_a), smp(q_a)

    n_pool = cache_k.shape[0]
    k_pages = jnp.transpose(cache_k, (0, 2, 3, 4, 1)).reshape(n_pool, HQK, PAGE_SIZE)
    v_pages = cache_v.reshape(n_pool, PAGE_SIZE * DIFF_HEADS, DIFF_V_DIM)
    dec = (page_table, k_pages, v_pages, q_s.reshape(bs, 1, HQK), k_s.reshape(bs, 1, HQK),
           v_s.reshape(bs, DIFF_HEADS, DIFF_V_DIM), lam_vecs, subln_w, lambda_init)
    z_m, xbc_m, dtr_m = _in_proj(hm, w_in, wdt, tm_p)
    y_m, conv_tail, ssm_p, _, o_s = _mamba_prompt(xbc_m, z_m, dtr_m, tail_meta[0], st_meta[0], conv_w, conv_b,
                                                  dt_bias, a_log, d_exp, norm_w, bp, seq // CHUNK, 0, dec)
    hm = _proj_ffn_ln(y_m, hm, w_out, ln(ln_mix_w, 0), ln(ln_mix_b, 0),
                      wg, wu, wd, 0, ln(ln_ffn_w, 0), ln(ln_ffn_b, 0), tm_p)
    kb_m, vb_m, qb_m, k_full, v_full = _kvq_prompt(hm, wk, wv, wq, *_rope_tables(N_META + jnp.arange(seq)),
                                                   k_a, v_a, tm_p, bp, seq)

    to3 = lambda t: t.reshape(bp, seq, D_MODEL)
    o_m = _attn_prompt(to3(qb_m), to3(kb_m), to3(vb_m), kb_a, vb_a, lam_vecs, subln_w, lambda_init, PAD_ROWS)
    hm = _proj_ffn_ln(o_m.reshape(n_main, D_MODEL), hm, wo, ln(ln_mix_w, 1), ln(ln_mix_b, 1),
                      wg, wu, wd, 1, ln(ln_ffn_w, 1), ln(ln_ffn_b, 1), tm_p)

    hs = _proj_ffn_ln(o_s.reshape(bs, D_MODEL), hs, wo, ln(ln_mix_w, 1), ln(ln_mix_b, 1),
                      wg, wu, wd, 1, ln(ln_ffn_w, 1), ln(ln_ffn_b, 1), bs)

    y_prompt = to3(hm)
    k_prompt = k_full.reshape(bp, N_META + seq, DIFF_HEADS, 2, DIFF_QK_DIM)
    v_prompt = v_full.reshape(bp, N_META + seq, DIFF_HEADS, DIFF_V_DIM)
    ssm_prompt = ssm_p[None]
    conv_prompt = conv_tail[None, :, SUBLANES - (D_CONV - 1):]
    y_sample = hs.reshape(bs, 1, D_MODEL)
    k_sample = k_s.reshape(bs, 1, DIFF_HEADS, 2, DIFF_QK_DIM)
    v_sample = v_s.reshape(bs, 1, DIFF_HEADS, DIFF_V_DIM)
    ssm_sample = ssm_s[None]
    conv_sample = jnp.transpose(conv_s, (1, 0, 2))[None]
    return (y_prompt, y_sample, k_prompt, v_prompt, ssm_prompt, conv_prompt, k_sample, v_sample, ssm_sample,
            conv_sample)
```
